```python
import math
import jax, jax.numpy as jnp
from jax import lax
import numpy as np

D_MODEL = 1024
BATCH = 8
SEQ = 4096
DEPTH = 4
DEC_BATCH = 4
DEC_SEQ = 4096
PAST_LEN = 128

N_HEADS = 8
HEAD_DIM = 64
V_DIM = 2 * HEAD_DIM
W_A = N_HEADS * V_DIM
QK_WIDTH = N_HEADS * 2 * HEAD_DIM
SCALE = HEAD_DIM ** -0.5
QBLK = 128
NUM_BUCKETS = 32
MAX_DISTANCE = 128
W_B = D_MODEL
CONV_W = 3
PROJ_WIDTHS = [QK_WIDTH, QK_WIDTH, W_A, W_A, W_B, W_B, W_B, W_B, D_MODEL, D_MODEL]
PROJ_SPLITS = [int(s) for s in np.cumsum(PROJ_WIDTHS)[:-1]]
N_PROJ = int(sum(PROJ_WIDTHS))
EPS = 1e-6

kernel_name = "hybrid_diffattn_shortconv_encoder"


def rmsnorm(x, g):
    xf = x.astype(jnp.float32)
    r = xf * lax.rsqrt(jnp.mean(xf * xf, axis=-1, keepdims=True) + EPS)
    return (r * g.astype(jnp.float32)).astype(x.dtype)


def lambda_init_fn(layer_idx):
    return 0.8 - 0.6 * math.exp(-0.3 * layer_idx)


def rel_bucket(rel):
    nb = NUM_BUCKETS // 2
    ret = (rel > 0).astype(jnp.int32) * nb
    n = jnp.abs(rel)
    max_exact = nb // 2
    is_small = n < max_exact
    nf = jnp.maximum(n, 1).astype(jnp.float32)
    large = max_exact + (jnp.log(nf / max_exact) / math.log(MAX_DISTANCE / max_exact) * (nb - max_exact)).astype(jnp.int32)
    large = jnp.minimum(large, nb - 1)
    return ret + jnp.where(is_small, n, large)


def diff_attention(q, k, v, rel_bias, q_norm_g, k_norm_g, lam):
    B, S, _ = q.shape
    q = rmsnorm(q.reshape(B, S, N_HEADS, 2, HEAD_DIM), q_norm_g) * SCALE
    k = rmsnorm(k.reshape(B, S, N_HEADS, 2, HEAD_DIM), k_norm_g)
    q1 = q[..., 0, :].transpose(0, 2, 1, 3)
    q2 = q[..., 1, :].transpose(0, 2, 1, 3)
    k1 = k[..., 0, :].transpose(0, 2, 1, 3)
    k2 = k[..., 1, :].transpose(0, 2, 1, 3)
    vh = v.reshape(B, S, N_HEADS, V_DIM).transpose(0, 2, 1, 3)
    nq = S // QBLK
    q1b = q1.reshape(B, N_HEADS, nq, QBLK, HEAD_DIM).transpose(2, 0, 1, 3, 4)
    q2b = q2.reshape(B, N_HEADS, nq, QBLK, HEAD_DIM).transpose(2, 0, 1, 3, 4)
    starts = jnp.arange(nq, dtype=jnp.int32) * QBLK
    kpos = jnp.arange(S, dtype=jnp.int32)
    qoff = jnp.arange(QBLK, dtype=jnp.int32)

    def block(args):
        q1i, q2i, st = args
        rel = kpos[None, :] - (st + qoff)[:, None]
        bias = rel_bias[rel_bucket(rel)].transpose(2, 0, 1).astype(jnp.float32)
        s1 = jnp.einsum("bhqd,bhkd->bhqk", q1i, k1).astype(jnp.float32) + bias
        s2 = jnp.einsum("bhqd,bhkd->bhqk", q2i, k2).astype(jnp.float32) + bias
        a = jax.nn.softmax(s1, axis=-1) - lam * jax.nn.softmax(s2, axis=-1)
        return jnp.einsum("bhqk,bhkv->bhqv", a.astype(vh.dtype), vh)

    o = lax.map(block, (q1b, q2b, starts))
    return o.transpose(1, 0, 3, 2, 4).reshape(B, S, N_HEADS, V_DIM)


def centred_conv3(u, w):
    up = jnp.pad(u, ((0, 0), (1, 1), (0, 0)))
    return up[:, :-2] * w[0] + up[:, 1:-1] * w[1] + up[:, 2:] * w[2]


def layer(x, layer_idx, rel_bias, norm_g, w_in, q_norm_g, k_norm_g, lambda_q1, lambda_k1,
          lambda_q2, lambda_k2, subln_g, w_attn_out, conv_w, w_conv_out, w_o):
    B, S, _ = x.shape
    h = rmsnorm(x, norm_g)
    proj = h @ w_in
    q, k, v, g_a, x_c, c_g, b_g, g_b, m_a, m_b = jnp.split(proj, PROJ_SPLITS, axis=-1)
    lam_init = lambda_init_fn(layer_idx)
    lam = (jnp.exp(jnp.sum(lambda_q1.astype(jnp.float32) * lambda_k1.astype(jnp.float32)))
           - jnp.exp(jnp.sum(lambda_q2.astype(jnp.float32) * lambda_k2.astype(jnp.float32)))
           + lam_init)
    o = diff_attention(q, k, v, rel_bias, q_norm_g, k_norm_g, lam)
    o = rmsnorm(o, subln_g) * (1.0 - lam_init)
    y_a = (o.reshape(B, S, W_A) * jax.nn.silu(g_a)) @ w_attn_out
    cv = centred_conv3(c_g * x_c, conv_w)
    y_b = (b_g * cv * jax.nn.silu(g_b)) @ w_conv_out
    merged = jax.nn.sigmoid(m_a) * y_a + jax.nn.sigmoid(m_b) * y_b
    return x + merged @ w_o


def trunk(x, rel_bias, norm_g, w_in, q_norm_g, k_norm_g, lambda_q1, lambda_k1, lambda_q2,
          lambda_k2, subln_g, w_attn_out, conv_w, w_conv_out, w_o):
    for l in range(DEPTH):
        x = layer(x, l, rel_bias, norm_g[l], w_in[l], q_norm_g[l], k_norm_g[l],
                  lambda_q1[l], lambda_k1[l], lambda_q2[l], lambda_k2[l], subln_g[l],
                  w_attn_out[l], conv_w[l], w_conv_out[l], w_o[l])
    return x


def setup_inputs(seed: int = 0) -> dict:
    key = jax.random.key(seed)
    ks = jax.random.split(key, 17)
    f32 = jnp.float32
    nrm = lambda k, s, sc: jax.random.normal(k, s, f32) * sc
    return {
        "x_prompt": nrm(ks[0], (BATCH, SEQ, D_MODEL), 1.0),
        "x_sample": nrm(ks[1], (DEC_BATCH, DEC_SEQ, D_MODEL), 1.0),
        "rel_bias": nrm(ks[2], (NUM_BUCKETS, N_HEADS), 0.3),
        "norm_g": 1.0 + nrm(ks[3], (DEPTH, D_MODEL), 0.02),
        "w_in": nrm(ks[4], (DEPTH, D_MODEL, N_PROJ), D_MODEL ** -0.5),
        "q_norm_g": 1.0 + nrm(ks[5], (DEPTH, HEAD_DIM), 0.02),
        "k_norm_g": 1.0 + nrm(ks[6], (DEPTH, HEAD_DIM), 0.02),
        "lambda_q1": nrm(ks[7], (DEPTH, HEAD_DIM), 0.1),
        "lambda_k1": nrm(ks[8], (DEPTH, HEAD_DIM), 0.1),
        "lambda_q2": nrm(ks[9], (DEPTH, HEAD_DIM), 0.1),
        "lambda_k2": nrm(ks[10], (DEPTH, HEAD_DIM), 0.1),
        "subln_g": 1.0 + nrm(ks[11], (DEPTH, V_DIM), 0.02),
        "w_attn_out": nrm(ks[12], (DEPTH, W_A, D_MODEL), W_A ** -0.5),
        "conv_w": nrm(ks[13], (DEPTH, CONV_W, W_B), CONV_W ** -0.5),
        "w_conv_out": nrm(ks[14], (DEPTH, W_B, D_MODEL), W_B ** -0.5),
        "w_o": nrm(ks[15], (DEPTH, D_MODEL, D_MODEL), D_MODEL ** -0.5),
    }


def reference(x_prompt, x_sample, rel_bias, norm_g, w_in, q_norm_g, k_norm_g, lambda_q1,
              lambda_k1, lambda_q2, lambda_k2, subln_g, w_attn_out, conv_w, w_conv_out, w_o):
    y_prompt = trunk(x_prompt, rel_bias, norm_g, w_in, q_norm_g, k_norm_g, lambda_q1, lambda_k1,
                     lambda_q2, lambda_k2, subln_g, w_attn_out, conv_w, w_conv_out, w_o)
    y_sample = trunk(x_sample, rel_bias, norm_g, w_in, q_norm_g, k_norm_g, lambda_q1, lambda_k1,
                     lambda_q2, lambda_k2, subln_g, w_attn_out, conv_w, w_conv_out, w_o)
    return (y_prompt, y_sample)
```

```python
import functools
import math

import jax
import jax.numpy as jnp
from jax import lax
from jax.experimental import pallas as pl
from jax.experimental.pallas import tpu as pltpu

D_MODEL = 1024
DEPTH = 4
N_HEADS = 8
HEAD_DIM = 64
V_DIM = 2 * HEAD_DIM
SCALE = HEAD_DIM ** -0.5
NUM_BUCKETS = 32
MAX_DISTANCE = 128
N_COLS = 10
EPS = 1e-6
LOG2E = 1.4426950408889634

LANES = 128
BF16_ROWS = 16
ATT_TILE = 256
PROJ_ROWS = 512
VMEM_LIMIT = 48 * 1024 * 1024

F32 = jnp.float32
BF16 = jnp.bfloat16


def _dot(a, b):
    return jnp.dot(a, b, preferred_element_type=F32)


def _dot_nt(a, b):
    return lax.dot_general(a, b, (((1,), (1,)), ((), ())), preferred_element_type=F32)


def _rel_bucket(rel):
    nb = NUM_BUCKETS // 2
    ret = (rel > 0).astype(jnp.int32) * nb
    n = jnp.abs(rel)
    max_exact = nb // 2
    is_small = n < max_exact
    nf = jnp.maximum(n, 1).astype(F32)
    large = max_exact + (jnp.log(nf / max_exact) / math.log(MAX_DISTANCE / max_exact)
                         * (nb - max_exact)).astype(jnp.int32)
    large = jnp.minimum(large, nb - 1)
    return ret + jnp.where(is_small, n, large)


def _bias_kernel(rb_ref, out_ref):
    h = pl.program_id(0)
    t = ATT_TILE
    row = lax.broadcasted_iota(jnp.int32, (t, t), 0)
    col = lax.broadcasted_iota(jnp.int32, (t, t), 1)
    for v in range(5):
        bucket = _rel_bucket(col - row + (v - 2) * t)
        val = jnp.zeros((t, t), F32)
        for b in range(NUM_BUCKETS):
            val = jnp.where(bucket == b, rb_ref[b, h], val)
        out_ref[0, v] = val * LOG2E


def _bias_tiles(rel_bias):
    t = ATT_TILE
    return pl.pallas_call(
        _bias_kernel,
        out_shape=jax.ShapeDtypeStruct((N_HEADS, 5, t, t), F32),
        grid=(N_HEADS,),
        in_specs=[pl.BlockSpec(memory_space=pltpu.SMEM)],
        out_specs=pl.BlockSpec((1, 5, t, t), lambda h: (h, 0, 0, 0)),
        compiler_params=pltpu.CompilerParams(dimension_semantics=("arbitrary",)),
        name="bias_tiles",
    )(rel_bias)


def _proj_kernel(x_ref, ng_ref, w_ref, qg_ref, kg_ref, o_ref, h_scr):
    n = pl.program_id(1)

    @pl.when(n == 0)
    def _():
        xf = x_ref[...]
        ms = jnp.mean(xf * xf, axis=-1, keepdims=True)
        h_scr[...] = (xf * lax.rsqrt(ms + EPS) * ng_ref[...]).astype(BF16)

    acc = _dot(h_scr[...], w_ref[...])

    @pl.when(n <= 1)
    def _():
        gi = lax.broadcasted_iota(jnp.int32, (LANES, LANES), 0) // HEAD_DIM
        gj = lax.broadcasted_iota(jnp.int32, (LANES, LANES), 1) // HEAD_DIM
        ones_bd = (gi == gj).astype(BF16)
        gain = jnp.where(n == 0, qg_ref[...] * (SCALE * LOG2E), kg_ref[...])
        for hd in range(N_HEADS):
            sl = slice(hd * LANES, (hd + 1) * LANES)
            blk = acc[:, sl]
            ss = _dot((blk * blk).astype(BF16), ones_bd)
            r = blk * lax.rsqrt(ss * (1.0 / HEAD_DIM) + EPS) * gain[:, sl]
            o_ref[:, sl] = r.astype(BF16)

    @pl.when((n == 3) | (n == 7))
    def _():
        o_ref[...] = (acc * jax.nn.sigmoid(acc)).astype(BF16)

    @pl.when(n >= 8)
    def _():
        o_ref[...] = jax.nn.sigmoid(acc).astype(BF16)

    @pl.when((n == 2) | ((n >= 4) & (n <= 6)))
    def _():
        o_ref[...] = acc.astype(BF16)


def _proj(x2d, norm_g, w_in, qg, kg):
    t = x2d.shape[0]
    tm = PROJ_ROWS
    return pl.pallas_call(
        _proj_kernel,
        out_shape=jax.ShapeDtypeStruct((t, N_COLS * D_MODEL), BF16),
        grid=(t // tm, N_COLS),
        in_specs=[
            pl.BlockSpec((tm, D_MODEL), lambda m, n: (m, 0)),
            pl.BlockSpec((1, D_MODEL), lambda m, n: (0, 0)),
            pl.BlockSpec((D_MODEL, D_MODEL), lambda m, n: (0, n)),
            pl.BlockSpec((1, D_MODEL), lambda m, n: (0, 0)),
            pl.BlockSpec((1, D_MODEL), lambda m, n: (0, 0)),
        ],
        out_specs=pl.BlockSpec((tm, D_MODEL), lambda m, n: (m, n)),
        scratch_shapes=[pltpu.VMEM((tm, D_MODEL), BF16)],
        compiler_params=pltpu.CompilerParams(
            dimension_semantics=("arbitrary", "arbitrary"), vmem_limit_bytes=VMEM_LIMIT),
        name="proj",
    )(x2d, norm_g, w_in, qg, kg)


def _lane_fold(x, op):
    out = x[:, 0:LANES]
    for c in range(1, x.shape[1] // LANES):
        out = op(out, x[:, c * LANES:(c + 1) * LANES])
    return out


def _attn_kernel(q_ref, k_ref, v_ref, bias_ref, lq1_ref, lk1_ref, lq2_ref, lk2_ref, sg_ref,
                 o_ref, s1_scr, s2_scr, p1_scr, p2_scr, *, lam_init, n_tiles):
    t = ATT_TILE
    i = pl.program_id(2)
    q = q_ref[...]
    first = lax.broadcasted_iota(jnp.int32, (t, LANES), 1) < HEAD_DIM
    zero = jnp.zeros_like(q)
    qa = jnp.where(first, q, zero)
    qb = jnp.where(first, zero, q)

    def scores(kt, carry):
        m1, m2 = carry
        kk = k_ref[pl.ds(pl.multiple_of(kt * t, t), t), :]
        b = bias_ref[0, jnp.clip(kt - i, -2, 2) + 2]
        s1 = _dot_nt(qa, kk) + b
        s2 = _dot_nt(qb, kk) + b
        s1_scr[kt] = s1
        s2_scr[kt] = s2
        return (jnp.maximum(m1, _lane_fold(s1, jnp.maximum)),
                jnp.maximum(m2, _lane_fold(s2, jnp.maximum)))

    neg = jnp.full((t, LANES), -jnp.inf, F32)
    m1, m2 = lax.fori_loop(0, n_tiles, scores, (neg, neg))
    m1 = jnp.max(m1, axis=-1, keepdims=True)
    m2 = jnp.max(m2, axis=-1, keepdims=True)

    def probs(kt, carry):
        l1, l2 = carry
        p1 = jnp.exp2(s1_scr[kt] - m1)
        p2 = jnp.exp2(s2_scr[kt] - m2)
        p1_scr[kt] = p1.astype(BF16)
        p2_scr[kt] = p2.astype(BF16)
        return l1 + _lane_fold(p1, jnp.add), l2 + _lane_fold(p2, jnp.add)

    zl = jnp.zeros((t, LANES), F32)
    l1, l2 = lax.fori_loop(0, n_tiles, probs, (zl, zl))
    l1 = jnp.sum(l1, axis=-1, keepdims=True)
    l2 = jnp.sum(l2, axis=-1, keepdims=True)

    lam = (jnp.exp(jnp.sum(lq1_ref[...] * lk1_ref[...], axis=-1, keepdims=True))
           - jnp.exp(jnp.sum(lq2_ref[...] * lk2_ref[...], axis=-1, keepdims=True))
           + lam_init)
    c1 = 1.0 / l1
    c2 = lam / l2

    def weighted(kt, acc):
        a = p1_scr[kt].astype(F32) * c1 - p2_scr[kt].astype(F32) * c2
        vv = v_ref[pl.ds(pl.multiple_of(kt * t, t), t), :]
        return acc + _dot(a.astype(BF16), vv)

    o = lax.fori_loop(0, n_tiles, weighted, jnp.zeros((t, V_DIM), F32))
    ms = jnp.mean(o * o, axis=-1, keepdims=True)
    o = o * lax.rsqrt(ms + EPS) * sg_ref[...] * (1.0 - lam_init)
    o_ref[...] = o.astype(BF16)


def _attn(proj, bias_tiles, lq1, lk1, lq2, lk2, subln_g, *, batch, seq, lam_init):
    t = ATT_TILE
    nq = seq // t
    small = lambda shape: pl.BlockSpec(shape, lambda b, h, i: (0, 0))
    return pl.pallas_call(
        functools.partial(_attn_kernel, lam_init=lam_init, n_tiles=nq),
        out_shape=jax.ShapeDtypeStruct((batch * seq, N_HEADS * V_DIM), BF16),
        grid=(batch, N_HEADS, nq),
        in_specs=[
            pl.BlockSpec((t, LANES), lambda b, h, i: (b * nq + i, h)),
            pl.BlockSpec((seq, LANES), lambda b, h, i: (b, N_HEADS + h)),
            pl.BlockSpec((seq, LANES), lambda b, h, i: (b, 2 * N_HEADS + h)),
            pl.BlockSpec((1, 5, t, t), lambda b, h, i: (h, 0, 0, 0)),
            small((1, HEAD_DIM)), small((1, HEAD_DIM)), small((1, HEAD_DIM)), small((1, HEAD_DIM)),
            small((1, V_DIM)),
        ],
        out_specs=pl.BlockSpec((t, V_DIM), lambda b, h, i: (b * nq + i, h)),
        scratch_shapes=[
            pltpu.VMEM((nq, t, t), F32), pltpu.VMEM((nq, t, t), F32),
            pltpu.VMEM((nq, t, t), BF16), pltpu.VMEM((nq, t, t), BF16),
        ],
        compiler_params=pltpu.CompilerParams(
            dimension_semantics=("arbitrary", "arbitrary", "arbitrary"),
            vmem_limit_bytes=VMEM_LIMIT),
        name="attn",
    )(proj, proj, proj, bias_tiles, lq1, lk1, lq2, lk2, subln_g)


def _post_kernel(o_ref, sga_ref, xc_ref, cg_ref, bg_ref, sgb_ref, sma_ref, smb_ref,
                 xcp_ref, cgp_ref, xcn_ref, cgn_ref, x_ref, cw_ref, wa_ref, wc_ref, wo_ref,
                 out_ref, *, tiles_per_seq):
    tm = out_ref.shape[0]
    pos = lax.rem(pl.program_id(0), tiles_per_seq)

    ua = (o_ref[...].astype(F32) * sga_ref[...].astype(F32)).astype(BF16)
    ya = _dot(ua, wa_ref[...])

    pc = cg_ref[...].astype(F32) * xc_ref[...].astype(F32)
    prev_row = (cgp_ref[...].astype(F32) * xcp_ref[...].astype(F32))[BF16_ROWS - 1:BF16_ROWS, :]
    next_row = (cgn_ref[...].astype(F32) * xcn_ref[...].astype(F32))[0:1, :]
    prev_row = jnp.where(pos == 0, 0.0, prev_row)
    next_row = jnp.where(pos == tiles_per_seq - 1, 0.0, next_row)
    row = lax.broadcasted_iota(jnp.int32, pc.shape, 0)
    up = jnp.where(row == 0, prev_row, pltpu.roll(pc, 1, axis=0))
    dn = jnp.where(row == tm - 1, next_row, pltpu.roll(pc, tm - 1, axis=0))
    cw = cw_ref[...]
    cv = up * cw[0:1, :] + pc * cw[1:2, :] + dn * cw[2:3, :]
    ub = (bg_ref[...].astype(F32) * cv * sgb_ref[...].astype(F32)).astype(BF16)
    yb = _dot(ub, wc_ref[...])

    merged = (sma_ref[...].astype(F32) * ya + smb_ref[...].astype(F32) * yb).astype(BF16)
    out_ref[...] = x_ref[...] + _dot(merged, wo_ref[...])


def _post(o, proj, x2d, conv_w, wa, wc, wo, *, seq):
    t = x2d.shape[0]
    tm = PROJ_ROWS
    hb = tm // BF16_ROWS
    n_hb = t // BF16_ROWS
    col = lambda c: pl.BlockSpec((tm, D_MODEL), lambda m: (m, c))
    prev = lambda c: pl.BlockSpec((BF16_ROWS, D_MODEL), lambda m: (jnp.maximum(m * hb - 1, 0), c))
    nxt = lambda c: pl.BlockSpec((BF16_ROWS, D_MODEL),
                                 lambda m: (jnp.minimum((m + 1) * hb, n_hb - 1), c))
    whole = lambda r: pl.BlockSpec((r, D_MODEL), lambda m: (0, 0))
    return pl.pallas_call(
        functools.partial(_post_kernel, tiles_per_seq=seq // tm),
        out_shape=jax.ShapeDtypeStruct((t, D_MODEL), F32),
        grid=(t // tm,),
        in_specs=[
            col(0),
            col(3), col(4), col(5), col(6), col(7), col(8), col(9),
            prev(4), prev(5), nxt(4), nxt(5),
            col(0),
            whole(3), whole(D_MODEL), whole(D_MODEL), whole(D_MODEL),
        ],
        out_specs=pl.BlockSpec((tm, D_MODEL), lambda m: (m, 0)),
        compiler_params=pltpu.CompilerParams(
            dimension_semantics=("arbitrary",), vmem_limit_bytes=VMEM_LIMIT),
        name="post",
    )(o, proj, proj, proj, proj, proj, proj, proj, proj, proj, proj, proj, x2d,
      conv_w, wa, wc, wo)


def _lambda_init(layer_idx):
    return 0.8 - 0.6 * math.exp(-0.3 * layer_idx)


def _trunk(x, bias_tiles, p):
    batch, seq, _ = x.shape
    x2d = x.reshape(batch * seq, D_MODEL)
    for l in range(DEPTH):
        proj = _proj(x2d, p["norm_g"][l], p["w_in"][l], p["qg"][l], p["kg"][l])
        o = _attn(proj, bias_tiles, p["lq1"][l], p["lk1"][l], p["lq2"][l], p["lk2"][l],
                  p["subln_g"][l], batch=batch, seq=seq, lam_init=_lambda_init(l))
        x2d = _post(o, proj, x2d, p["conv_w"][l], p["wa"][l], p["wc"][l], p["wo"][l], seq=seq)
    return x2d.reshape(batch, seq, D_MODEL)


def kernel(x_prompt, x_sample, rel_bias, norm_g, w_in, q_norm_g, k_norm_g, lambda_q1, lambda_k1,
           lambda_q2, lambda_k2, subln_g, w_attn_out, conv_w, w_conv_out, w_o):
    reps = D_MODEL // HEAD_DIM
    p = {
        "norm_g": norm_g.reshape(DEPTH, 1, D_MODEL),
        "w_in": w_in.astype(BF16),
        "qg": jnp.tile(q_norm_g, (1, reps)).reshape(DEPTH, 1, D_MODEL),
        "kg": jnp.tile(k_norm_g, (1, reps)).reshape(DEPTH, 1, D_MODEL),
        "lq1": lambda_q1.reshape(DEPTH, 1, HEAD_DIM),
        "lk1": lambda_k1.reshape(DEPTH, 1, HEAD_DIM),
        "lq2": lambda_q2.reshape(DEPTH, 1, HEAD_DIM),
        "lk2": lambda_k2.reshape(DEPTH, 1, HEAD_DIM),
        "subln_g": subln_g.reshape(DEPTH, 1, V_DIM),
        "conv_w": conv_w,
        "wa": w_attn_out.astype(BF16),
        "wc": w_conv_out.astype(BF16),
        "wo": w_o.astype(BF16),
    }
    bias_tiles = _bias_tiles(rel_bias)
    return (_trunk(x_prompt, bias_tiles, p), _trunk(x_sample, bias_tiles, p))
```

```python
import functools
import math

import jax
import jax.numpy as jnp
from jax import lax
from jax.experimental import pallas as pl
from jax.experimental.pallas import tpu as pltpu

D_MODEL = 1024
DEPTH = 4
N_HEADS = 8
HEAD_DIM = 64
V_DIM = 2 * HEAD_DIM
SCALE = HEAD_DIM ** -0.5
NUM_BUCKETS = 32
MAX_DISTANCE = 128
N_COLS = 10
EPS = 1e-6
LOG2E = 1.4426950408889634
FAR_NEG_BUCKET = NUM_BUCKETS // 2 - 1
FAR_POS_BUCKET = NUM_BUCKETS - 1

LANES = 128
BF16_ROWS = 16
ATT_TILE = 256
ATT_UNROLL = 4
PROJ_ROWS = 512
VMEM_LIMIT = 48 * 1024 * 1024

F32 = jnp.float32
BF16 = jnp.bfloat16


def _dot(a, b):
    return jnp.dot(a, b, preferred_element_type=F32)


def _dot_nt(a, b):
    return lax.dot_general(a, b, (((1,), (1,)), ((), ())), preferred_element_type=F32)


def _rel_bucket(rel):
    nb = NUM_BUCKETS // 2
    ret = (rel > 0).astype(jnp.int32) * nb
    n = jnp.abs(rel)
    max_exact = nb // 2
    is_small = n < max_exact
    nf = jnp.maximum(n, 1).astype(F32)
    large = max_exact + (jnp.log(nf / max_exact) / math.log(MAX_DISTANCE / max_exact)
                         * (nb - max_exact)).astype(jnp.int32)
    large = jnp.minimum(large, nb - 1)
    return ret + jnp.where(is_small, n, large)


def _bias_kernel(rb_ref, out_ref):
    h = pl.program_id(0)
    t = ATT_TILE
    row = lax.broadcasted_iota(jnp.int32, (t, t), 0)
    col = lax.broadcasted_iota(jnp.int32, (t, t), 1)
    for v in range(3):
        bucket = _rel_bucket(col - row + (v - 1) * t)
        val = jnp.zeros((t, t), F32)
        for b in range(NUM_BUCKETS):
            val = jnp.where(bucket == b, rb_ref[b, h], val)
        far = rb_ref[FAR_NEG_BUCKET, h] if v == 0 else rb_ref[FAR_POS_BUCKET, h]
        out_ref[0, v] = (val - far) * LOG2E
    out_ref[0, 3] = jnp.zeros((t, t), F32)


def _bias_tiles(rel_bias):
    t = ATT_TILE
    return pl.pallas_call(
        _bias_kernel,
        out_shape=jax.ShapeDtypeStruct((N_HEADS, 4, t, t), F32),
        grid=(N_HEADS,),
        in_specs=[pl.BlockSpec(memory_space=pltpu.SMEM)],
        out_specs=pl.BlockSpec((1, 4, t, t), lambda h: (h, 0, 0, 0)),
        compiler_params=pltpu.CompilerParams(dimension_semantics=("arbitrary",)),
        name="bias_tiles",
    )(rel_bias)


def _proj_kernel(x_ref, ng_ref, w_ref, qg_ref, kg_ref, o_ref, k2_ref, h_scr):
    n = pl.program_id(1)

    @pl.when(n == 0)
    def _():
        xf = x_ref[...]
        ms = jnp.mean(xf * xf, axis=-1, keepdims=True)
        h_scr[...] = (xf * lax.rsqrt(ms + EPS) * ng_ref[...]).astype(BF16)

    acc = _dot(h_scr[...], w_ref[...])

    def qk_norm(hd, gain):
        gi = lax.broadcasted_iota(jnp.int32, (LANES, LANES), 0) // HEAD_DIM
        gj = lax.broadcasted_iota(jnp.int32, (LANES, LANES), 1) // HEAD_DIM
        ones_bd = (gi == gj).astype(BF16)
        sl = slice(hd * LANES, (hd + 1) * LANES)
        blk = acc[:, sl]
        ss = _dot((blk * blk).astype(BF16), ones_bd)
        return (blk * lax.rsqrt(ss * (1.0 / HEAD_DIM) + EPS) * gain[:, sl]).astype(BF16)

    @pl.when(n == 0)
    def _():
        gain = qg_ref[...] * (SCALE * LOG2E)
        for hd in range(N_HEADS):
            o_ref[:, hd * LANES:(hd + 1) * LANES] = qk_norm(hd, gain)

    @pl.when(n == 1)
    def _():
        lane = lax.broadcasted_iota(jnp.int32, (acc.shape[0], LANES), 1)
        aug1 = jnp.where((lane >> 1) == HEAD_DIM // 2, 1.0, 0.0).astype(BF16)
        aug2 = jnp.where(lane < 2, 1.0, 0.0).astype(BF16)
        for hd in range(N_HEADS):
            sl = slice(hd * LANES, (hd + 1) * LANES)
            kb = qk_norm(hd, kg_ref[...])
            o_ref[:, sl] = jnp.where(lane < HEAD_DIM, kb, aug1)
            k2_ref[:, sl] = jnp.where(lane < HEAD_DIM, aug2, kb)

    @pl.when((n == 3) | (n == 7))
    def _():
        o_ref[...] = (acc * jax.nn.sigmoid(acc)).astype(BF16)

    @pl.when(n >= 8)
    def _():
        o_ref[...] = jax.nn.sigmoid(acc).astype(BF16)

    @pl.when((n == 2) | ((n >= 4) & (n <= 6)))
    def _():
        o_ref[...] = acc.astype(BF16)


def _proj(x2d, norm_g, w_in, qg, kg):
    t = x2d.shape[0]
    tm = PROJ_ROWS
    return pl.pallas_call(
        _proj_kernel,
        out_shape=(jax.ShapeDtypeStruct((t, N_COLS * D_MODEL), BF16),
                   jax.ShapeDtypeStruct((t, D_MODEL), BF16)),
        grid=(t // tm, N_COLS),
        in_specs=[
            pl.BlockSpec((tm, D_MODEL), lambda m, n: (m, 0)),
            pl.BlockSpec((1, D_MODEL), lambda m, n: (0, 0)),
            pl.BlockSpec((D_MODEL, D_MODEL), lambda m, n: (0, n)),
            pl.BlockSpec((1, D_MODEL), lambda m, n: (0, 0)),
            pl.BlockSpec((1, D_MODEL), lambda m, n: (0, 0)),
        ],
        out_specs=(pl.BlockSpec((tm, D_MODEL), lambda m, n: (m, n)),
                   pl.BlockSpec((tm, D_MODEL), lambda m, n: (m, 0))),
        scratch_shapes=[pltpu.VMEM((tm, D_MODEL), BF16)],
        compiler_params=pltpu.CompilerParams(
            dimension_semantics=("arbitrary", "arbitrary"), vmem_limit_bytes=VMEM_LIMIT),
        name="proj",
    )(x2d, norm_g, w_in, qg, kg)


def _attn_kernel(rb_ref, q_ref, k1_ref, k2_ref, v_ref, dl_ref, lq1_ref, lk1_ref, lq2_ref, lk2_ref,
                 sg_ref, o_ref, s_scr, p_scr, q_scr, m_scr, c_scr, *, lam_init, nq, n_blocks):
    t = ATT_TILE
    nc = t // LANES
    step = pl.program_id(0)

    @pl.when(step == 0)
    def _():
        s_scr[...] = jnp.zeros(s_scr.shape, F32)
        p_scr[...] = jnp.zeros(p_scr.shape, BF16)
        m_scr[...] = jnp.zeros(m_scr.shape, F32)
        c_scr[...] = jnp.zeros(c_scr.shape, F32)

    ta = jnp.minimum(step, n_blocks - 1)
    ia = lax.rem(ta, nq)
    ha = lax.rem(ta // nq, N_HEADS)

    q = q_ref[...]
    lane = lax.broadcasted_iota(jnp.int32, (t, LANES), 1)
    for side, bucket in enumerate((FAR_NEG_BUCKET, FAR_POS_BUCKET)):
        cv = jnp.full((t, LANES), rb_ref[bucket, ha] * LOG2E, F32)
        hi = cv.astype(BF16).astype(F32)
        for comp in range(2):
            base = HEAD_DIM if comp == 0 else 0
            keep = (lane < HEAD_DIM) if comp == 0 else (lane >= HEAD_DIM)
            aug = jnp.where(lane == base, hi, jnp.where(lane == base + 1, cv - hi, 0.0))
            q_scr[comp, side] = jnp.where(keep, q, aug.astype(BF16))

    mb1 = m_scr[0]
    mb2 = m_scr[1]
    c1 = c_scr[0]
    rho = c_scr[1].astype(BF16)

    def fold(x, op):
        out = x[:, 0:LANES]
        for c in range(1, nc):
            out = op(out, x[:, c * LANES:(c + 1) * LANES])
        return out

    def tile(kt, carry):
        m1, m2, l1, l2, acc = carry
        rows = pl.ds(pl.multiple_of(kt * t, t), t)
        lanes = [slice(c * LANES, (c + 1) * LANES) for c in range(nc)]
        a = jnp.concatenate([p_scr[0, kt, :, sl] - p_scr[1, kt, :, sl] * rho for sl in lanes], axis=1)
        acc = acc + _dot(a, v_ref[rows, :])
        for sl in lanes:
            p1 = jnp.exp2(s_scr[0, kt, :, sl] - mb1)
            p2 = jnp.exp2(s_scr[1, kt, :, sl] - mb2)
            p_scr[0, kt, :, sl] = p1.astype(BF16)
            p_scr[1, kt, :, sl] = p2.astype(BF16)
            l1 = l1 + p1
            l2 = l2 + p2
        d = kt - ia
        dl = dl_ref[0, jnp.where((d >= -1) & (d <= 1), d + 1, 3)]
        side = jnp.where(d < 0, 0, 1)
        s1 = _dot_nt(q_scr[0, side], k1_ref[rows, :]) + dl
        s2 = _dot_nt(q_scr[1, side], k2_ref[rows, :]) + dl
        s_scr[0, kt] = s1
        s_scr[1, kt] = s2
        m1 = jnp.maximum(m1, fold(s1, jnp.maximum))
        m2 = jnp.maximum(m2, fold(s2, jnp.maximum))
        return m1, m2, l1, l2, acc

    neg = jnp.full((t, LANES), -jnp.inf, F32)
    zl = jnp.zeros((t, LANES), F32)
    m1, m2, l1, l2, acc = lax.fori_loop(
        0, nq, tile, (neg, neg, zl, zl, jnp.zeros((t, V_DIM), F32)), unroll=ATT_UNROLL)

    m_scr[0] = jnp.broadcast_to(jnp.max(m1, axis=-1, keepdims=True), (t, LANES))
    m_scr[1] = jnp.broadcast_to(jnp.max(m2, axis=-1, keepdims=True), (t, LANES))

    lam = (jnp.exp(jnp.sum(lq1_ref[...] * lk1_ref[...], axis=-1, keepdims=True))
           - jnp.exp(jnp.sum(lq2_ref[...] * lk2_ref[...], axis=-1, keepdims=True))
           + lam_init)
    l1 = jnp.sum(l1, axis=-1, keepdims=True)
    l2 = jnp.sum(l2, axis=-1, keepdims=True)
    c_scr[0] = jnp.broadcast_to(1.0 / l1, (t, LANES))
    c_scr[1] = jnp.broadcast_to(lam * l1 / l2, (t, LANES))

    o = acc * c1
    ms = jnp.mean(o * o, axis=-1, keepdims=True)
    o = o * lax.rsqrt(ms + EPS) * sg_ref[...] * (1.0 - lam_init)
    o_ref[...] = o.astype(BF16)


def _attn(rel_bias, proj, k2a, bias_tiles, lq1, lk1, lq2, lk2, subln_g, *, batch, seq, lam_init):
    t = ATT_TILE
    nq = seq // t
    assert nq % ATT_UNROLL == 0
    n_blocks = batch * N_HEADS * nq

    def split(blk):
        bh = blk // nq
        return bh // N_HEADS, lax.rem(bh, N_HEADS), lax.rem(blk, nq)

    def a_blk(s):
        return split(jnp.minimum(s, n_blocks - 1))

    def c_blk(s):
        return split(jnp.clip(s - 2, 0, n_blocks - 1))

    def q_map(s):
        b, h, i = a_blk(s)
        return b * nq + i, h

    def k1_map(s):
        b, h, _ = a_blk(s)
        return b, N_HEADS + h

    def k2_map(s):
        b, h, _ = a_blk(s)
        return b, h

    def v_map(s):
        b, h, _ = c_blk(s)
        return b, 2 * N_HEADS + h

    def dl_map(s):
        return a_blk(s)[1], 0, 0, 0

    def o_map(s):
        b, h, i = c_blk(s)
        return b * nq + i, h

    small = lambda shape: pl.BlockSpec(shape, lambda s: (0, 0))
    return pl.pallas_call(
        functools.partial(_attn_kernel, lam_init=lam_init, nq=nq, n_blocks=n_blocks),
        out_shape=jax.ShapeDtypeStruct((batch * seq, N_HEADS * V_DIM), BF16),
        grid=(n_blocks + 2,),
        in_specs=[
            pl.BlockSpec(memory_space=pltpu.SMEM),
            pl.BlockSpec((t, LANES), q_map),
            pl.BlockSpec((seq, LANES), k1_map),
            pl.BlockSpec((seq, LANES), k2_map),
            pl.BlockSpec((seq, LANES), v_map),
            pl.BlockSpec((1, 4, t, t), dl_map),
            small((1, HEAD_DIM)), small((1, HEAD_DIM)), small((1, HEAD_DIM)), small((1, HEAD_DIM)),
            small((1, V_DIM)),
        ],
        out_specs=pl.BlockSpec((t, V_DIM), o_map),
        scratch_shapes=[
            pltpu.VMEM((2, nq, t, t), F32),
            pltpu.VMEM((2, nq, t, t), BF16),
            pltpu.VMEM((2, 2, t, LANES), BF16),
            pltpu.VMEM((2, t, LANES), F32),
            pltpu.VMEM((2, t, LANES), F32),
        ],
        compiler_params=pltpu.CompilerParams(
            dimension_semantics=("arbitrary",), vmem_limit_bytes=VMEM_LIMIT),
        name="attn",
    )(rel_bias, proj, proj, k2a, proj, bias_tiles, lq1, lk1, lq2, lk2, subln_g)


def _post_kernel(o_ref, sga_ref, xc_ref, cg_ref, bg_ref, sgb_ref, sma_ref, smb_ref,
                 xcp_ref, cgp_ref, xcn_ref, cgn_ref, x_ref, cw_ref, wa_ref, wc_ref, wo_ref,
                 out_ref, *, tiles_per_seq):
    tm = out_ref.shape[0]
    pos = lax.rem(pl.program_id(0), tiles_per_seq)

    ua = (o_ref[...].astype(F32) * sga_ref[...].astype(F32)).astype(BF16)
    ya = _dot(ua, wa_ref[...])

    pc = cg_ref[...].astype(F32) * xc_ref[...].astype(F32)
    prev_row = (cgp_ref[...].astype(F32) * xcp_ref[...].astype(F32))[BF16_ROWS - 1:BF16_ROWS, :]
    next_row = (cgn_ref[...].astype(F32) * xcn_ref[...].astype(F32))[0:1, :]
    prev_row = jnp.where(pos == 0, 0.0, prev_row)
    next_row = jnp.where(pos == tiles_per_seq - 1, 0.0, next_row)
    row = lax.broadcasted_iota(jnp.int32, pc.shape, 0)
    up = jnp.where(row == 0, prev_row, pltpu.roll(pc, 1, axis=0))
    dn = jnp.where(row == tm - 1, next_row, pltpu.roll(pc, tm - 1, axis=0))
    cw = cw_ref[...]
    cv = up * cw[0:1, :] + pc * cw[1:2, :] + dn * cw[2:3, :]
    ub = (bg_ref[...].astype(F32) * cv * sgb_ref[...].astype(F32)).astype(BF16)
    yb = _dot(ub, wc_ref[...])

    merged = (sma_ref[...].astype(F32) * ya + smb_ref[...].astype(F32) * yb).astype(BF16)
    out_ref[...] = x_ref[...] + _dot(merged, wo_ref[...])


def _post(o, proj, x2d, conv_w, wa, wc, wo, *, seq):
    t = x2d.shape[0]
    tm = PROJ_ROWS
    hb = tm // BF16_ROWS
    n_hb = t // BF16_ROWS
    col = lambda c: pl.BlockSpec((tm, D_MODEL), lambda m: (m, c))
    prev = lambda c: pl.BlockSpec((BF16_ROWS, D_MODEL), lambda m: (jnp.maximum(m * hb - 1, 0), c))
    nxt = lambda c: pl.BlockSpec((BF16_ROWS, D_MODEL),
                                 lambda m: (jnp.minimum((m + 1) * hb, n_hb - 1), c))
    whole = lambda r: pl.BlockSpec((r, D_MODEL), lambda m: (0, 0))
    return pl.pallas_call(
        functools.partial(_post_kernel, tiles_per_seq=seq // tm),
        out_shape=jax.ShapeDtypeStruct((t, D_MODEL), F32),
        grid=(t // tm,),
        in_specs=[
            col(0),
            col(3), col(4), col(5), col(6), col(7), col(8), col(9),
            prev(4), prev(5), nxt(4), nxt(5),
            col(0),
            whole(3), whole(D_MODEL), whole(D_MODEL), whole(D_MODEL),
        ],
        out_specs=pl.BlockSpec((tm, D_MODEL), lambda m: (m, 0)),
        compiler_params=pltpu.CompilerParams(
            dimension_semantics=("arbitrary",), vmem_limit_bytes=VMEM_LIMIT),
        name="post",
    )(o, proj, proj, proj, proj, proj, proj, proj, proj, proj, proj, proj, x2d,
      conv_w, wa, wc, wo)


def _lambda_init(layer_idx):
    return 0.8 - 0.6 * math.exp(-0.3 * layer_idx)


def _trunk(x, bias_tiles, p):
    batch, seq, _ = x.shape
    x2d = x.reshape(batch * seq, D_MODEL)
    for l in range(DEPTH):
        proj, k2a = _proj(x2d, p["norm_g"][l], p["w_in"][l], p["qg"][l], p["kg"][l])
        o = _attn(p["rel_bias"], proj, k2a, bias_tiles, p["lq1"][l], p["lk1"][l], p["lq2"][l],
                  p["lk2"][l], p["subln_g"][l], batch=batch, seq=seq, lam_init=_lambda_init(l))
        x2d = _post(o, proj, x2d, p["conv_w"][l], p["wa"][l], p["wc"][l], p["wo"][l], seq=seq)
    return x2d.reshape(batch, seq, D_MODEL)


def kernel(x_prompt, x_sample, rel_bias, norm_g, w_in, q_norm_g, k_norm_g, lambda_q1, lambda_k1,
           lambda_q2, lambda_k2, subln_g, w_attn_out, conv_w, w_conv_out, w_o):
    reps = D_MODEL // HEAD_DIM
    p = {
        "rel_bias": rel_bias,
        "norm_g": norm_g.reshape(DEPTH, 1, D_MODEL),
        "w_in": w_in.astype(BF16),
        "qg": jnp.tile(q_norm_g, (1, reps)).reshape(DEPTH, 1, D_MODEL),
        "kg": jnp.tile(k_norm_g, (1, reps)).reshape(DEPTH, 1, D_MODEL),
        "lq1": lambda_q1.reshape(DEPTH, 1, HEAD_DIM),
        "lk1": lambda_k1.reshape(DEPTH, 1, HEAD_DIM),
        "lq2": lambda_q2.reshape(DEPTH, 1, HEAD_DIM),
        "lk2": lambda_k2.reshape(DEPTH, 1, HEAD_DIM),
        "subln_g": subln_g.reshape(DEPTH, 1, V_DIM),
        "conv_w": conv_w,
        "wa": w_attn_out.astype(BF16),
        "wc": w_conv_out.astype(BF16),
        "wo": w_o.astype(BF16),
    }
    bias_tiles = _bias_tiles(rel_bias)
    return (_trunk(x_prompt, bias_tiles, p), _trunk(x_sample, bias_tiles, p))
```

```python
import functools
import math

import jax
import jax.numpy as jnp
from jax import lax
from jax.experimental import pallas as pl
from jax.experimental.pallas import tpu as pltpu

D_MODEL = 1024
DEPTH = 4
N_HEADS = 8
HEAD_DIM = 64
V_DIM = 2 * HEAD_DIM
SCALE = HEAD_DIM ** -0.5
NUM_BUCKETS = 32
MAX_DISTANCE = 128
N_COLS = 10
EPS = 1e-6
LOG2E = 1.4426950408889634
FAR_NEG_BUCKET = NUM_BUCKETS // 2 - 1
FAR_POS_BUCKET = NUM_BUCKETS - 1

LANES = 128
BF16_ROWS = 16
F32_ROWS = 8
MXU_COLS = 256
PROJ_CHUNK = 2 * MXU_COLS
ATT_TILE = 512
ATT_UNROLL = 8
PROJ_ROWS = 512
VMEM_LIMIT = 56 * 1024 * 1024

F32 = jnp.float32
BF16 = jnp.bfloat16


def _dot(a, b):
    return jnp.dot(a, b, preferred_element_type=F32)


def _dot_nt(a, b):
    return lax.dot_general(a, b, (((1,), (1,)), ((), ())), preferred_element_type=F32)


def _rel_bucket(rel):
    nb = NUM_BUCKETS // 2
    ret = (rel > 0).astype(jnp.int32) * nb
    n = jnp.abs(rel)
    max_exact = nb // 2
    is_small = n < max_exact
    nf = jnp.maximum(n, 1).astype(F32)
    large = max_exact + (jnp.log(nf / max_exact) / math.log(MAX_DISTANCE / max_exact)
                         * (nb - max_exact)).astype(jnp.int32)
    large = jnp.minimum(large, nb - 1)
    return ret + jnp.where(is_small, n, large)


def _bias_kernel(rb_ref, out_ref):
    h = pl.program_id(0)
    t = ATT_TILE
    key = lax.broadcasted_iota(jnp.int32, (t, t), 0)
    qry = lax.broadcasted_iota(jnp.int32, (t, t), 1)
    for v in range(3):
        bucket = _rel_bucket(key - qry + (v - 1) * t)
        val = jnp.zeros((t, t), F32)
        for b in range(NUM_BUCKETS):
            val = jnp.where(bucket == b, rb_ref[b, h], val)
        far = rb_ref[FAR_NEG_BUCKET, h] if v == 0 else rb_ref[FAR_POS_BUCKET, h]
        out_ref[0, v] = (val - far) * LOG2E
    out_ref[0, 3] = jnp.zeros((t, t), F32)


def _bias_tiles(rel_bias):
    t = ATT_TILE
    return pl.pallas_call(
        _bias_kernel,
        out_shape=jax.ShapeDtypeStruct((N_HEADS, 4, t, t), F32),
        grid=(N_HEADS,),
        in_specs=[pl.BlockSpec(memory_space=pltpu.SMEM)],
        out_specs=pl.BlockSpec((1, 4, t, t), lambda h: (h, 0, 0, 0)),
        compiler_params=pltpu.CompilerParams(
            dimension_semantics=("arbitrary",), vmem_limit_bytes=VMEM_LIMIT),
        name="bias_tiles",
    )(rel_bias)


def _proj_kernel(x_ref, ng_ref, w_ref, qg_ref, kg_ref, o_ref, k2_ref, vt_ref, h_scr):
    n = pl.program_id(1)

    @pl.when(n == 0)
    def _():
        xf = x_ref[...]
        ms = jnp.mean(xf * xf, axis=-1, keepdims=True)
        h_scr[...] = (xf * lax.rsqrt(ms + EPS) * ng_ref[...]).astype(BF16)

    rows = h_scr.shape[0]
    def per_chunk(epilogue):
        h = h_scr[...]
        for c in range(D_MODEL // PROJ_CHUNK):
            acc = _dot(h, w_ref[:, c * PROJ_CHUNK:(c + 1) * PROJ_CHUNK])
            for j in range(PROJ_CHUNK // MXU_COLS):
                lo = c * PROJ_CHUNK + j * MXU_COLS
                epilogue(acc[:, j * MXU_COLS:(j + 1) * MXU_COLS], slice(lo, lo + MXU_COLS))

    def qk_norm(acc, gain):
        gi = lax.broadcasted_iota(jnp.int32, (MXU_COLS, MXU_COLS), 0) // HEAD_DIM
        gj = lax.broadcasted_iota(jnp.int32, (MXU_COLS, MXU_COLS), 1) // HEAD_DIM
        ones_bd = (gi == gj).astype(BF16)
        ss = _dot((acc * acc).astype(BF16), ones_bd)
        return (acc * lax.rsqrt(ss * (1.0 / HEAD_DIM) + EPS) * gain).astype(BF16)

    @pl.when(n == 0)
    def _():
        def q_out(acc, cs):
            o_ref[:, cs] = qk_norm(acc, qg_ref[:, cs] * (SCALE * LOG2E))
        per_chunk(q_out)

    @pl.when(n == 1)
    def _():
        lane = lax.broadcasted_iota(jnp.int32, (rows, MXU_COLS), 1) & (LANES - 1)
        aug1 = jnp.where((lane >> 1) == HEAD_DIM // 2, 1.0, 0.0).astype(BF16)
        aug2 = jnp.where(lane < 2, 1.0, 0.0).astype(BF16)

        def k_out(acc, cs):
            kb = qk_norm(acc, kg_ref[:, cs])
            o_ref[:, cs] = jnp.where(lane < HEAD_DIM, kb, aug1)
            k2_ref[:, cs] = jnp.where(lane < HEAD_DIM, aug2, kb)
        per_chunk(k_out)

    @pl.when(n == 2)
    def _():
        def v_out(acc, cs):
            o_ref[:, cs] = acc.astype(BF16)
            vt_ref[cs, :] = acc.T.astype(BF16)
        per_chunk(v_out)

    @pl.when((n == 3) | (n == 7))
    def _():
        def silu_out(acc, cs):
            o_ref[:, cs] = (acc * jax.nn.sigmoid(acc)).astype(BF16)
        per_chunk(silu_out)

    @pl.when((n >= 4) & (n <= 6))
    def _():
        def plain_out(acc, cs):
            o_ref[:, cs] = acc.astype(BF16)
        per_chunk(plain_out)

    @pl.when(n >= 8)
    def _():
        def sigmoid_out(acc, cs):
            o_ref[:, cs] = jax.nn.sigmoid(acc).astype(BF16)
        per_chunk(sigmoid_out)


def _proj(x2d, norm_g, w_in, qg, kg):
    t = x2d.shape[0]
    tm = PROJ_ROWS
    return pl.pallas_call(
        _proj_kernel,
        out_shape=(jax.ShapeDtypeStruct((t, N_COLS * D_MODEL), BF16),
                   jax.ShapeDtypeStruct((t, D_MODEL), BF16),
                   jax.ShapeDtypeStruct((D_MODEL, t), BF16)),
        grid=(t // tm, N_COLS),
        in_specs=[
            pl.BlockSpec((tm, D_MODEL), lambda m, n: (m, 0)),
            pl.BlockSpec((1, D_MODEL), lambda m, n: (0, 0)),
            pl.BlockSpec((D_MODEL, D_MODEL), lambda m, n: (0, n)),
            pl.BlockSpec((1, D_MODEL), lambda m, n: (0, 0)),
            pl.BlockSpec((1, D_MODEL), lambda m, n: (0, 0)),
        ],
        out_specs=(pl.BlockSpec((tm, D_MODEL), lambda m, n: (m, n)),
                   pl.BlockSpec((tm, D_MODEL), lambda m, n: (m, 0)),
                   pl.BlockSpec((D_MODEL, tm), lambda m, n: (0, m))),
        scratch_shapes=[pltpu.VMEM((tm, D_MODEL), BF16)],
        compiler_params=pltpu.CompilerParams(
            dimension_semantics=("arbitrary", "arbitrary"), vmem_limit_bytes=VMEM_LIMIT),
        name="proj",
    )(x2d, norm_g, w_in, qg, kg)


def _attn_kernel(rb_ref, q_ref, k1_ref, k2_ref, vt_ref, dl_ref, lq1_ref, lk1_ref, lq2_ref, lk2_ref,
                 sg_ref, o_ref, s_scr, p_scr, q_scr, m_scr, c_scr, *, lam_init, nq, n_blocks):
    t = ATT_TILE
    step = pl.program_id(0)

    @pl.when(step == 0)
    def _():
        s_scr[...] = jnp.zeros(s_scr.shape, F32)
        p_scr[...] = jnp.zeros(p_scr.shape, BF16)
        m_scr[...] = jnp.zeros(m_scr.shape, F32)
        c_scr[...] = jnp.zeros(c_scr.shape, F32)

    ta = jnp.minimum(step, n_blocks - 1)
    ia = lax.rem(ta, nq)
    ha = lax.rem(ta // nq, N_HEADS)

    q = q_ref[...]
    lane = lax.broadcasted_iota(jnp.int32, (t, LANES), 1)
    for side, bucket in enumerate((FAR_NEG_BUCKET, FAR_POS_BUCKET)):
        cv = jnp.full((t, LANES), rb_ref[bucket, ha] * LOG2E, F32)
        hi = cv.astype(BF16).astype(F32)
        for comp in range(2):
            base = HEAD_DIM if comp == 0 else 0
            keep = (lane < HEAD_DIM) if comp == 0 else (lane >= HEAD_DIM)
            aug = jnp.where(lane == base, hi, jnp.where(lane == base + 1, cv - hi, 0.0))
            q_scr[comp, side] = jnp.where(keep, q, aug.astype(BF16))

    mb1 = m_scr[0]
    mb2 = m_scr[1]
    c1 = c_scr[0]
    rho = jnp.concatenate([c_scr[1], c_scr[1]], axis=0).astype(BF16)

    def slabs(x, rows):
        return x.reshape(x.shape[0] // rows, rows, x.shape[1])

    def tile(kt, carry):
        m1, m2, l1, l2, acc = carry
        keys = pl.ds(pl.multiple_of(kt * t, t), t)
        a = slabs(p_scr[0, kt], BF16_ROWS) - slabs(p_scr[1, kt], BF16_ROWS) * rho
        acc = acc + _dot(vt_ref[:, keys], a.reshape(t, t))
        p1 = jnp.exp2(slabs(s_scr[0, kt], F32_ROWS) - mb1)
        p2 = jnp.exp2(slabs(s_scr[1, kt], F32_ROWS) - mb2)
        p_scr[0, kt] = p1.reshape(t, t).astype(BF16)
        p_scr[1, kt] = p2.reshape(t, t).astype(BF16)
        l1 = l1 + jnp.sum(p1, axis=0)
        l2 = l2 + jnp.sum(p2, axis=0)
        d = kt - ia
        dl = dl_ref[0, jnp.where((d >= -1) & (d <= 1), d + 1, 3)]
        side = jnp.where(d < 0, 0, 1)
        s1 = _dot_nt(k1_ref[keys, :], q_scr[0, side]) + dl
        s2 = _dot_nt(k2_ref[keys, :], q_scr[1, side]) + dl
        s_scr[0, kt] = s1
        s_scr[1, kt] = s2
        m1 = jnp.maximum(m1, jnp.max(slabs(s1, F32_ROWS), axis=0))
        m2 = jnp.maximum(m2, jnp.max(slabs(s2, F32_ROWS), axis=0))
        return m1, m2, l1, l2, acc

    neg = jnp.full((F32_ROWS, t), -jnp.inf, F32)
    zl = jnp.zeros((F32_ROWS, t), F32)
    m1, m2, l1, l2, acc = lax.fori_loop(
        0, nq, tile, (neg, neg, zl, zl, jnp.zeros((V_DIM, t), F32)), unroll=ATT_UNROLL)

    m_scr[0] = jnp.broadcast_to(jnp.max(m1, axis=0, keepdims=True), (F32_ROWS, t))
    m_scr[1] = jnp.broadcast_to(jnp.max(m2, axis=0, keepdims=True), (F32_ROWS, t))

    lam = (jnp.exp(jnp.sum(lq1_ref[...] * lk1_ref[...], axis=-1, keepdims=True))
           - jnp.exp(jnp.sum(lq2_ref[...] * lk2_ref[...], axis=-1, keepdims=True))
           + lam_init)
    l1 = jnp.sum(l1, axis=0, keepdims=True)
    l2 = jnp.sum(l2, axis=0, keepdims=True)
    c_scr[0] = jnp.broadcast_to(1.0 / l1, (F32_ROWS, t))
    c_scr[1] = jnp.broadcast_to(lam * l1 / l2, (F32_ROWS, t))

    ot = (slabs(acc, F32_ROWS) * c1).reshape(V_DIM, t)
    ms = jnp.mean(ot * ot, axis=0, keepdims=True)
    ot = ot * lax.rsqrt(ms + EPS) * sg_ref[...] * (1.0 - lam_init)
    o_ref[...] = ot.T.astype(BF16)


def _attn(rel_bias, proj, k2a, vt, bias_tiles, lq1, lk1, lq2, lk2, subln_g, *, batch, seq, lam_init):
    t = ATT_TILE
    nq = seq // t
    assert nq % ATT_UNROLL == 0
    n_blocks = batch * N_HEADS * nq

    def split(blk):
        bh = blk // nq
        return bh // N_HEADS, lax.rem(bh, N_HEADS), lax.rem(blk, nq)

    def a_blk(s):
        return split(jnp.minimum(s, n_blocks - 1))

    def c_blk(s):
        return split(jnp.clip(s - 2, 0, n_blocks - 1))

    def q_map(s):
        b, h, i = a_blk(s)
        return b * nq + i, h

    def k1_map(s):
        b, h, _ = a_blk(s)
        return b, N_HEADS + h

    def k2_map(s):
        b, h, _ = a_blk(s)
        return b, h

    def v_map(s):
        b, h, _ = c_blk(s)
        return h, b

    def dl_map(s):
        return a_blk(s)[1], 0, 0, 0

    def o_map(s):
        b, h, i = c_blk(s)
        return b * nq + i, h

    small = lambda shape: pl.BlockSpec(shape, lambda s: (0, 0))
    return pl.pallas_call(
        functools.partial(_attn_kernel, lam_init=lam_init, nq=nq, n_blocks=n_blocks),
        out_shape=jax.ShapeDtypeStruct((batch * seq, N_HEADS * V_DIM), BF16),
        grid=(n_blocks + 2,),
        in_specs=[
            pl.BlockSpec(memory_space=pltpu.SMEM),
            pl.BlockSpec((t, LANES), q_map),
            pl.BlockSpec((seq, LANES), k1_map),
            pl.BlockSpec((seq, LANES), k2_map),
            pl.BlockSpec((V_DIM, seq), v_map),
            pl.BlockSpec((1, 4, t, t), dl_map),
            small((1, HEAD_DIM)), small((1, HEAD_DIM)), small((1, HEAD_DIM)), small((1, HEAD_DIM)),
            small((V_DIM, 1)),
        ],
        out_specs=pl.BlockSpec((t, V_DIM), o_map),
        scratch_shapes=[
            pltpu.VMEM((2, nq, t, t), F32),
            pltpu.VMEM((2, nq, t, t), BF16),
            pltpu.VMEM((2, 2, t, LANES), BF16),
            pltpu.VMEM((2, F32_ROWS, t), F32),
            pltpu.VMEM((2, F32_ROWS, t), F32),
        ],
        compiler_params=pltpu.CompilerParams(
            dimension_semantics=("arbitrary",), vmem_limit_bytes=VMEM_LIMIT),
        name="attn",
    )(rel_bias, proj, proj, k2a, vt, bias_tiles, lq1, lk1, lq2, lk2, subln_g)


def _post_kernel(o_ref, sga_ref, xc_ref, cg_ref, bg_ref, sgb_ref, sma_ref, smb_ref,
                 xcp_ref, cgp_ref, xcn_ref, cgn_ref, x_ref, cw_ref, wa_ref, wc_ref, wo_ref,
                 out_ref, *, tiles_per_seq):
    tm = out_ref.shape[0]
    pos = lax.rem(pl.program_id(0), tiles_per_seq)

    ua = (o_ref[...].astype(F32) * sga_ref[...].astype(F32)).astype(BF16)
    ya = _dot(ua, wa_ref[...])

    pc = cg_ref[...].astype(F32) * xc_ref[...].astype(F32)
    prev_row = (cgp_ref[...].astype(F32) * xcp_ref[...].astype(F32))[BF16_ROWS - 1:BF16_ROWS, :]
    next_row = (cgn_ref[...].astype(F32) * xcn_ref[...].astype(F32))[0:1, :]
    prev_row = jnp.where(pos == 0, 0.0, prev_row)
    next_row = jnp.where(pos == tiles_per_seq - 1, 0.0, next_row)
    row = lax.broadcasted_iota(jnp.int32, pc.shape, 0)
    up = jnp.where(row == 0, prev_row, pltpu.roll(pc, 1, axis=0))
    dn = jnp.where(row == tm - 1, next_row, pltpu.roll(pc, tm - 1, axis=0))
    cw = cw_ref[...]
    cv = up * cw[0:1, :] + pc * cw[1:2, :] + dn * cw[2:3, :]
    ub = (bg_ref[...].astype(F32) * cv * sgb_ref[...].astype(F32)).astype(BF16)
    yb = _dot(ub, wc_ref[...])

    merged = (sma_ref[...].astype(F32) * ya + smb_ref[...].astype(F32) * yb).astype(BF16)
    out_ref[...] = x_ref[...] + _dot(merged, wo_ref[...])


def _post(o, proj, x2d, conv_w, wa, wc, wo, *, seq):
    t = x2d.shape[0]
    tm = PROJ_ROWS
    hb = tm // BF16_ROWS
    n_hb = t // BF16_ROWS
    col = lambda c: pl.BlockSpec((tm, D_MODEL), lambda m: (m, c))
    prev = lambda c: pl.BlockSpec((BF16_ROWS, D_MODEL), lambda m: (jnp.maximum(m * hb - 1, 0), c))
    nxt = lambda c: pl.BlockSpec((BF16_ROWS, D_MODEL),
                                 lambda m: (jnp.minimum((m + 1) * hb, n_hb - 1), c))
    whole = lambda r: pl.BlockSpec((r, D_MODEL), lambda m: (0, 0))
    return pl.pallas_call(
        functools.partial(_post_kernel, tiles_per_seq=seq // tm),
        out_shape=jax.ShapeDtypeStruct((t, D_MODEL), F32),
        grid=(t // tm,),
        in_specs=[
            col(0),
            col(3), col(4), col(5), col(6), col(7), col(8), col(9),
            prev(4), prev(5), nxt(4), nxt(5),
            col(0),
            whole(3), whole(D_MODEL), whole(D_MODEL), whole(D_MODEL),
        ],
        out_specs=pl.BlockSpec((tm, D_MODEL), lambda m: (m, 0)),
        compiler_params=pltpu.CompilerParams(
            dimension_semantics=("arbitrary",), vmem_limit_bytes=VMEM_LIMIT),
        name="post",
    )(o, proj, proj, proj, proj, proj, proj, proj, proj, proj, proj, proj, x2d,
      conv_w, wa, wc, wo)


def _lambda_init(layer_idx):
    return 0.8 - 0.6 * math.exp(-0.3 * layer_idx)


def _trunk(x, bias_tiles, p):
    batch, seq, _ = x.shape
    x2d = x.reshape(batch * seq, D_MODEL)
    for l in range(DEPTH):
        proj, k2a, vt = _proj(x2d, p["norm_g"][l], p["w_in"][l], p["qg"][l], p["kg"][l])
        o = _attn(p["rel_bias"], proj, k2a, vt, bias_tiles, p["lq1"][l], p["lk1"][l], p["lq2"][l],
                  p["lk2"][l], p["subln_g"][l], batch=batch, seq=seq, lam_init=_lambda_init(l))
        x2d = _post(o, proj, x2d, p["conv_w"][l], p["wa"][l], p["wc"][l], p["wo"][l], seq=seq)
    return x2d.reshape(batch, seq, D_MODEL)


def kernel(x_prompt, x_sample, rel_bias, norm_g, w_in, q_norm_g, k_norm_g, lambda_q1, lambda_k1,
           lambda_q2, lambda_k2, subln_g, w_attn_out, conv_w, w_conv_out, w_o):
    reps = D_MODEL // HEAD_DIM
    p = {
        "rel_bias": rel_bias,
        "norm_g": norm_g.reshape(DEPTH, 1, D_MODEL),
        "w_in": w_in.astype(BF16),
        "qg": jnp.tile(q_norm_g, (1, reps)).reshape(DEPTH, 1, D_MODEL),
        "kg": jnp.tile(k_norm_g, (1, reps)).reshape(DEPTH, 1, D_MODEL),
        "lq1": lambda_q1.reshape(DEPTH, 1, HEAD_DIM),
        "lk1": lambda_k1.reshape(DEPTH, 1, HEAD_DIM),
        "lq2": lambda_q2.reshape(DEPTH, 1, HEAD_DIM),
        "lk2": lambda_k2.reshape(DEPTH, 1, HEAD_DIM),
        "subln_g": subln_g.reshape(DEPTH, V_DIM, 1),
        "conv_w": conv_w,
        "wa": w_attn_out.astype(BF16),
        "wc": w_conv_out.astype(BF16),
        "wo": w_o.astype(BF16),
    }
    bias_tiles = _bias_tiles(rel_bias)
    return (_trunk(x_prompt, bias_tiles, p), _trunk(x_sample, bias_tiles, p))
```

```python
import functools
import math

import jax
import jax.numpy as jnp
from jax import lax
from jax.experimental import pallas as pl
from jax.experimental.pallas import tpu as pltpu

D_MODEL = 1024
DEPTH = 4
N_HEADS = 8
HEAD_DIM = 64
V_DIM = 2 * HEAD_DIM
SCALE = HEAD_DIM ** -0.5
NUM_BUCKETS = 32
MAX_DISTANCE = 128
N_COLS = 10
EPS = 1e-6
LOG2E = 1.4426950408889634
FAR_NEG_BUCKET = NUM_BUCKETS // 2 - 1
FAR_POS_BUCKET = NUM_BUCKETS - 1

LANES = 128
BF16_ROWS = 16
F32_ROWS = 8
MXU_COLS = 256
PROJ_CHUNK = 2 * MXU_COLS
PROJ_CHUNK_ROWS = 512
ATT_TILE = 512
PROJ_ROWS = 1024
POST_ROWS = 512
VMEM_LIMIT = 56 * 1024 * 1024

F32 = jnp.float32
BF16 = jnp.bfloat16


def _dot(a, b):
    return jnp.dot(a, b, preferred_element_type=F32)


def _dot_nt(a, b):
    return lax.dot_general(a, b, (((1,), (1,)), ((), ())), preferred_element_type=F32)


def _rel_bucket(rel):
    nb = NUM_BUCKETS // 2
    ret = (rel > 0).astype(jnp.int32) * nb
    n = jnp.abs(rel)
    max_exact = nb // 2
    is_small = n < max_exact
    nf = jnp.maximum(n, 1).astype(F32)
    large = max_exact + (jnp.log(nf / max_exact) / math.log(MAX_DISTANCE / max_exact)
                         * (nb - max_exact)).astype(jnp.int32)
    large = jnp.minimum(large, nb - 1)
    return ret + jnp.where(is_small, n, large)


def _bias_kernel(rb_ref, out_ref):
    h = pl.program_id(0)
    t = ATT_TILE
    key = lax.broadcasted_iota(jnp.int32, (t, t), 0)
    qry = lax.broadcasted_iota(jnp.int32, (t, t), 1)
    for v in range(3):
        bucket = _rel_bucket(key - qry + (v - 1) * t)
        val = jnp.zeros((t, t), F32)
        for b in range(NUM_BUCKETS):
            val = jnp.where(bucket == b, rb_ref[b, h], val)
        far = rb_ref[FAR_NEG_BUCKET, h] if v == 0 else rb_ref[FAR_POS_BUCKET, h]
        out_ref[0, v] = (val - far) * LOG2E
    out_ref[0, 3] = jnp.zeros((t, t), F32)


def _bias_tiles(rel_bias):
    t = ATT_TILE
    return pl.pallas_call(
        _bias_kernel,
        out_shape=jax.ShapeDtypeStruct((N_HEADS, 4, t, t), F32),
        grid=(N_HEADS,),
        in_specs=[pl.BlockSpec(memory_space=pltpu.SMEM)],
        out_specs=pl.BlockSpec((1, 4, t, t), lambda h: (h, 0, 0, 0)),
        compiler_params=pltpu.CompilerParams(
            dimension_semantics=("arbitrary",), vmem_limit_bytes=VMEM_LIMIT),
        name="bias_tiles",
    )(rel_bias)


def _proj_kernel(x_ref, ng_ref, w_ref, qg_ref, kg_ref, o_ref, k2_ref, vt_ref, h_scr):
    n = pl.program_id(1)

    @pl.when(n == 0)
    def _():
        xf = x_ref[...]
        ms = jnp.mean(xf * xf, axis=-1, keepdims=True)
        h_scr[...] = (xf * lax.rsqrt(ms + EPS) * ng_ref[...]).astype(BF16)

    rows = h_scr.shape[0]

    def per_chunk(epilogue):
        for r0 in range(0, rows, PROJ_CHUNK_ROWS):
            rs = slice(r0, r0 + PROJ_CHUNK_ROWS)
            h = h_scr[rs, :]
            for c in range(D_MODEL // PROJ_CHUNK):
                acc = _dot(h, w_ref[:, c * PROJ_CHUNK:(c + 1) * PROJ_CHUNK])
                for j in range(PROJ_CHUNK // MXU_COLS):
                    lo = c * PROJ_CHUNK + j * MXU_COLS
                    epilogue(acc[:, j * MXU_COLS:(j + 1) * MXU_COLS], rs, slice(lo, lo + MXU_COLS))

    def qk_norm(acc, gain):
        gi = lax.broadcasted_iota(jnp.int32, (MXU_COLS, MXU_COLS), 0) // HEAD_DIM
        gj = lax.broadcasted_iota(jnp.int32, (MXU_COLS, MXU_COLS), 1) // HEAD_DIM
        ones_bd = (gi == gj).astype(BF16)
        ss = _dot((acc * acc).astype(BF16), ones_bd)
        return (acc * lax.rsqrt(ss * (1.0 / HEAD_DIM) + EPS) * gain).astype(BF16)

    @pl.when(n == 0)
    def _():
        def q_out(acc, rs, cs):
            o_ref[rs, cs] = qk_norm(acc, qg_ref[:, cs] * (SCALE * LOG2E))
        per_chunk(q_out)

    @pl.when(n == 1)
    def _():
        lane = lax.broadcasted_iota(jnp.int32, (PROJ_CHUNK_ROWS, MXU_COLS), 1) & (LANES - 1)
        aug1 = jnp.where((lane >> 1) == HEAD_DIM // 2, 1.0, 0.0).astype(BF16)
        aug2 = jnp.where(lane < 2, 1.0, 0.0).astype(BF16)

        def k_out(acc, rs, cs):
            kb = qk_norm(acc, kg_ref[:, cs])
            o_ref[rs, cs] = jnp.where(lane < HEAD_DIM, kb, aug1)
            k2_ref[rs, cs] = jnp.where(lane < HEAD_DIM, aug2, kb)
        per_chunk(k_out)

    @pl.when(n == 2)
    def _():
        def v_out(acc, rs, cs):
            o_ref[rs, cs] = acc.astype(BF16)
            vt_ref[cs, rs] = acc.T.astype(BF16)
        per_chunk(v_out)

    @pl.when((n == 3) | (n == 7))
    def _():
        def silu_out(acc, rs, cs):
            o_ref[rs, cs] = (acc * jax.nn.sigmoid(acc)).astype(BF16)
        per_chunk(silu_out)

    @pl.when((n >= 4) & (n <= 6))
    def _():
        def plain_out(acc, rs, cs):
            o_ref[rs, cs] = acc.astype(BF16)
        per_chunk(plain_out)

    @pl.when(n >= 8)
    def _():
        def sigmoid_out(acc, rs, cs):
            o_ref[rs, cs] = jax.nn.sigmoid(acc).astype(BF16)
        per_chunk(sigmoid_out)


def _proj(x2d, norm_g, w_in, qg, kg):
    t = x2d.shape[0]
    tm = PROJ_ROWS
    return pl.pallas_call(
        _proj_kernel,
        out_shape=(jax.ShapeDtypeStruct((t, N_COLS * D_MODEL), BF16),
                   jax.ShapeDtypeStruct((t, D_MODEL), BF16),
                   jax.ShapeDtypeStruct((D_MODEL, t), BF16)),
        grid=(t // tm, N_COLS),
        in_specs=[
            pl.BlockSpec((tm, D_MODEL), lambda m, n: (m, 0)),
            pl.BlockSpec((1, D_MODEL), lambda m, n: (0, 0)),
            pl.BlockSpec((D_MODEL, D_MODEL), lambda m, n: (0, n)),
            pl.BlockSpec((1, D_MODEL), lambda m, n: (0, 0)),
            pl.BlockSpec((1, D_MODEL), lambda m, n: (0, 0)),
        ],
        out_specs=(pl.BlockSpec((tm, D_MODEL), lambda m, n: (m, n)),
                   pl.BlockSpec((tm, D_MODEL), lambda m, n: (m, 0)),
                   pl.BlockSpec((D_MODEL, tm), lambda m, n: (0, m))),
        scratch_shapes=[pltpu.VMEM((tm, D_MODEL), BF16)],
        compiler_params=pltpu.CompilerParams(
            dimension_semantics=("arbitrary", "arbitrary"), vmem_limit_bytes=VMEM_LIMIT),
        name="proj",
    )(x2d, norm_g, w_in, qg, kg)


def _attn_kernel(rb_ref, q_ref, k1_ref, k2_ref, vt_ref, dl_ref, lq1_ref, lk1_ref, lq2_ref, lk2_ref,
                 sg_ref, o_ref, s_scr, p_scr, q_scr, m_scr, c_scr, *, lam_init, nq, n_blocks):
    t = ATT_TILE
    step = pl.program_id(0)

    @pl.when(step == 0)
    def _():
        s_scr[...] = jnp.zeros(s_scr.shape, F32)
        p_scr[...] = jnp.zeros(p_scr.shape, BF16)
        m_scr[...] = jnp.zeros(m_scr.shape, F32)
        c_scr[...] = jnp.zeros(c_scr.shape, F32)

    ta = jnp.minimum(step, n_blocks - 1)
    ia = lax.rem(ta, nq)
    ha = lax.rem(ta // nq, N_HEADS)
    ic = lax.rem(jnp.clip(step - 2, 0, n_blocks - 1), nq)

    q = q_ref[...]
    lane = lax.broadcasted_iota(jnp.int32, (t, LANES), 1)
    for side, bucket in enumerate((FAR_NEG_BUCKET, FAR_POS_BUCKET)):
        cv = jnp.full((t, LANES), rb_ref[bucket, ha] * LOG2E, F32)
        hi = cv.astype(BF16).astype(F32)
        for comp in range(2):
            base = HEAD_DIM if comp == 0 else 0
            keep = (lane < HEAD_DIM) if comp == 0 else (lane >= HEAD_DIM)
            aug = jnp.where(lane == base, hi, jnp.where(lane == base + 1, cv - hi, 0.0))
            q_scr[comp, side] = jnp.where(keep, q, aug.astype(BF16))

    mb1 = m_scr[0]
    mb2 = m_scr[1]
    c1 = c_scr[0]
    rho = jnp.concatenate([c_scr[1], c_scr[1]], axis=0).astype(BF16)

    def slabs(x, rows):
        return x.reshape(x.shape[0] // rows, rows, x.shape[1])

    def key_tile(i, r):
        kt = i + r
        kt = jnp.where(kt >= nq, kt - nq, kt)
        return pl.ds(pl.multiple_of(kt * t, t), t)

    def tile(r, carry):
        m1, m2, l1, l2, acc = carry
        a = slabs(p_scr[0, r], BF16_ROWS) - slabs(p_scr[1, r], BF16_ROWS) * rho
        acc = acc + _dot(vt_ref[:, key_tile(ic, r)], a.reshape(t, t))
        p1 = jnp.exp2(slabs(s_scr[0, r], F32_ROWS) - mb1)
        p2 = jnp.exp2(slabs(s_scr[1, r], F32_ROWS) - mb2)
        p_scr[0, r] = p1.reshape(t, t).astype(BF16)
        p_scr[1, r] = p2.reshape(t, t).astype(BF16)
        l1 = l1 + jnp.sum(p1, axis=0)
        l2 = l2 + jnp.sum(p2, axis=0)
        wrapped = ia + r >= nq
        side = 1 if r == 0 else jnp.where(wrapped, 0, 1)
        keys = key_tile(ia, r)
        s1 = _dot_nt(k1_ref[keys, :], q_scr[0, side])
        s2 = _dot_nt(k2_ref[keys, :], q_scr[1, side])
        if r in (0, 1, nq - 1):
            offset = (r + 1) % nq - 1
            dl = dl_ref[0, jnp.where(wrapped == (r == nq - 1), offset + 1, 3)]
            s1 = s1 + dl
            s2 = s2 + dl
        s_scr[0, r] = s1
        s_scr[1, r] = s2
        m1 = jnp.maximum(m1, jnp.max(slabs(s1, F32_ROWS), axis=0))
        m2 = jnp.maximum(m2, jnp.max(slabs(s2, F32_ROWS), axis=0))
        return m1, m2, l1, l2, acc

    neg = jnp.full((F32_ROWS, t), -jnp.inf, F32)
    zl = jnp.zeros((F32_ROWS, t), F32)
    carry = (neg, neg, zl, zl, jnp.zeros((V_DIM, t), F32))
    for r in range(nq):
        carry = tile(r, carry)
    m1, m2, l1, l2, acc = carry

    m_scr[0] = jnp.broadcast_to(jnp.max(m1, axis=0, keepdims=True), (F32_ROWS, t))
    m_scr[1] = jnp.broadcast_to(jnp.max(m2, axis=0, keepdims=True), (F32_ROWS, t))

    lam = (jnp.exp(jnp.sum(lq1_ref[...] * lk1_ref[...], axis=-1, keepdims=True))
           - jnp.exp(jnp.sum(lq2_ref[...] * lk2_ref[...], axis=-1, keepdims=True))
           + lam_init)
    l1 = jnp.sum(l1, axis=0, keepdims=True)
    l2 = jnp.sum(l2, axis=0, keepdims=True)
    c_scr[0] = jnp.broadcast_to(1.0 / l1, (F32_ROWS, t))
    c_scr[1] = jnp.broadcast_to(lam * l1 / l2, (F32_ROWS, t))

    ot = (slabs(acc, F32_ROWS) * c1).reshape(V_DIM, t)
    ms = jnp.mean(ot * ot, axis=0, keepdims=True)
    ot = ot * lax.rsqrt(ms + EPS) * sg_ref[...] * (1.0 - lam_init)
    o_ref[...] = ot.T.astype(BF16)


def _attn(rel_bias, proj, k2a, vt, bias_tiles, lq1, lk1, lq2, lk2, subln_g, *, batch, seq, lam_init):
    t = ATT_TILE
    nq = seq // t
    assert nq >= 3
    n_blocks = batch * N_HEADS * nq

    def split(blk):
        bh = blk // nq
        return bh // N_HEADS, lax.rem(bh, N_HEADS), lax.rem(blk, nq)

    def a_blk(s):
        return split(jnp.minimum(s, n_blocks - 1))

    def c_blk(s):
        return split(jnp.clip(s - 2, 0, n_blocks - 1))

    def q_map(s):
        b, h, i = a_blk(s)
        return b * nq + i, h

    def k1_map(s):
        b, h, _ = a_blk(s)
        return b, N_HEADS + h

    def k2_map(s):
        b, h, _ = a_blk(s)
        return b, h

    def v_map(s):
        b, h, _ = c_blk(s)
        return h, b

    def dl_map(s):
        return a_blk(s)[1], 0, 0, 0

    def o_map(s):
        b, h, i = c_blk(s)
        return b * nq + i, h

    small = lambda shape: pl.BlockSpec(shape, lambda s: (0, 0))
    return pl.pallas_call(
        functools.partial(_attn_kernel, lam_init=lam_init, nq=nq, n_blocks=n_blocks),
        out_shape=jax.ShapeDtypeStruct((batch * seq, N_HEADS * V_DIM), BF16),
        grid=(n_blocks + 2,),
        in_specs=[
            pl.BlockSpec(memory_space=pltpu.SMEM),
            pl.BlockSpec((t, LANES), q_map),
            pl.BlockSpec((seq, LANES), k1_map),
            pl.BlockSpec((seq, LANES), k2_map),
            pl.BlockSpec((V_DIM, seq), v_map),
            pl.BlockSpec((1, 4, t, t), dl_map),
            small((1, HEAD_DIM)), small((1, HEAD_DIM)), small((1, HEAD_DIM)), small((1, HEAD_DIM)),
            small((V_DIM, 1)),
        ],
        out_specs=pl.BlockSpec((t, V_DIM), o_map),
        scratch_shapes=[
            pltpu.VMEM((2, nq, t, t), F32),
            pltpu.VMEM((2, nq, t, t), BF16),
            pltpu.VMEM((2, 2, t, LANES), BF16),
            pltpu.VMEM((2, F32_ROWS, t), F32),
            pltpu.VMEM((2, F32_ROWS, t), F32),
        ],
        compiler_params=pltpu.CompilerParams(
            dimension_semantics=("arbitrary",), vmem_limit_bytes=VMEM_LIMIT),
        name="attn",
    )(rel_bias, proj, proj, k2a, vt, bias_tiles, lq1, lk1, lq2, lk2, subln_g)


def _post_kernel(o_ref, sga_ref, xc_ref, cg_ref, bg_ref, sgb_ref, sma_ref, smb_ref,
                 xcp_ref, cgp_ref, xcn_ref, cgn_ref, x_ref, cw_ref, wa_ref, wc_ref, wo_ref,
                 out_ref, *, tiles_per_seq):
    tm = out_ref.shape[0]
    pos = lax.rem(pl.program_id(0), tiles_per_seq)

    ua = (o_ref[...].astype(F32) * sga_ref[...].astype(F32)).astype(BF16)
    ya = _dot(ua, wa_ref[...])

    pc = cg_ref[...].astype(F32) * xc_ref[...].astype(F32)
    prev_row = (cgp_ref[...].astype(F32) * xcp_ref[...].astype(F32))[BF16_ROWS - 1:BF16_ROWS, :]
    next_row = (cgn_ref[...].astype(F32) * xcn_ref[...].astype(F32))[0:1, :]
    prev_row = jnp.where(pos == 0, 0.0, prev_row)
    next_row = jnp.where(pos == tiles_per_seq - 1, 0.0, next_row)
    row = lax.broadcasted_iota(jnp.int32, pc.shape, 0)
    up = jnp.where(row == 0, prev_row, pltpu.roll(pc, 1, axis=0))
    dn = jnp.where(row == tm - 1, next_row, pltpu.roll(pc, tm - 1, axis=0))
    cw = cw_ref[...]
    cv = up * cw[0:1, :] + pc * cw[1:2, :] + dn * cw[2:3, :]
    ub = (bg_ref[...].astype(F32) * cv * sgb_ref[...].astype(F32)).astype(BF16)
    yb = _dot(ub, wc_ref[...])

    merged = (sma_ref[...].astype(F32) * ya + smb_ref[...].astype(F32) * yb).astype(BF16)
    out_ref[...] = x_ref[...] + _dot(merged, wo_ref[...])


def _post(o, proj, x2d, conv_w, wa, wc, wo, *, seq):
    t = x2d.shape[0]
    tm = POST_ROWS
    hb = tm // BF16_ROWS
    n_hb = t // BF16_ROWS
    col = lambda c: pl.BlockSpec((tm, D_MODEL), lambda m: (m, c))
    prev = lambda c: pl.BlockSpec((BF16_ROWS, D_MODEL), lambda m: (jnp.maximum(m * hb - 1, 0), c))
    nxt = lambda c: pl.BlockSpec((BF16_ROWS, D_MODEL),
                                 lambda m: (jnp.minimum((m + 1) * hb, n_hb - 1), c))
    whole = lambda r: pl.BlockSpec((r, D_MODEL), lambda m: (0, 0))
    return pl.pallas_call(
        functools.partial(_post_kernel, tiles_per_seq=seq // tm),
        out_shape=jax.ShapeDtypeStruct((t, D_MODEL), F32),
        grid=(t // tm,),
        in_specs=[
            col(0),
            col(3), col(4), col(5), col(6), col(7), col(8), col(9),
            prev(4), prev(5), nxt(4), nxt(5),
            col(0),
            whole(3), whole(D_MODEL), whole(D_MODEL), whole(D_MODEL),
        ],
        out_specs=pl.BlockSpec((tm, D_MODEL), lambda m: (m, 0)),
        compiler_params=pltpu.CompilerParams(
            dimension_semantics=("arbitrary",), vmem_limit_bytes=VMEM_LIMIT),
        name="post",
    )(o, proj, proj, proj, proj, proj, proj, proj, proj, proj, proj, proj, x2d,
      conv_w, wa, wc, wo)


def _lambda_init(layer_idx):
    return 0.8 - 0.6 * math.exp(-0.3 * layer_idx)


def _trunk(x, bias_tiles, p):
    batch, seq, _ = x.shape
    x2d = x.reshape(batch * seq, D_MODEL)
    for l in range(DEPTH):
        proj, k2a, vt = _proj(x2d, p["norm_g"][l], p["w_in"][l], p["qg"][l], p["kg"][l])
        o = _attn(p["rel_bias"], proj, k2a, vt, bias_tiles, p["lq1"][l], p["lk1"][l], p["lq2"][l],
                  p["lk2"][l], p["subln_g"][l], batch=batch, seq=seq, lam_init=_lambda_init(l))
        x2d = _post(o, proj, x2d, p["conv_w"][l], p["wa"][l], p["wc"][l], p["wo"][l], seq=seq)
    return x2d.reshape(batch, seq, D_MODEL)


def kernel(x_prompt, x_sample, rel_bias, norm_g, w_in, q_norm_g, k_norm_g, lambda_q1, lambda_k1,
           lambda_q2, lambda_k2, subln_g, w_attn_out, conv_w, w_conv_out, w_o):
    reps = D_MODEL // HEAD_DIM
    p = {
        "rel_bias": rel_bias,
        "norm_g": norm_g.reshape(DEPTH, 1, D_MODEL),
        "w_in": w_in.astype(BF16),
        "qg": jnp.tile(q_norm_g, (1, reps)).reshape(DEPTH, 1, D_MODEL),
        "kg": jnp.tile(k_norm_g, (1, reps)).reshape(DEPTH, 1, D_MODEL),
        "lq1": lambda_q1.reshape(DEPTH, 1, HEAD_DIM),
        "lk1": lambda_k1.reshape(DEPTH, 1, HEAD_DIM),
        "lq2": lambda_q2.reshape(DEPTH, 1, HEAD_DIM),
        "lk2": lambda_k2.reshape(DEPTH, 1, HEAD_DIM),
        "subln_g": subln_g.reshape(DEPTH, V_DIM, 1),
        "conv_w": conv_w,
        "wa": w_attn_out.astype(BF16),
        "wc": w_conv_out.astype(BF16),
        "wo": w_o.astype(BF16),
    }
    bias_tiles = _bias_tiles(rel_bias)
    return (_trunk(x_prompt, bias_tiles, p), _trunk(x_sample, bias_tiles, p))
```

```python
import functools
import math

import jax
import jax.numpy as jnp
from jax import lax
from jax.experimental import pallas as pl
from jax.experimental.pallas import tpu as pltpu

D_MODEL = 1024
DEPTH = 4
N_HEADS = 8
HEAD_DIM = 64
V_DIM = 2 * HEAD_DIM
SCALE = HEAD_DIM ** -0.5
NUM_BUCKETS = 32
MAX_DISTANCE = 128
N_COLS = 10
EPS = 1e-6
LOG2E = 1.4426950408889634
FAR_NEG_BUCKET = NUM_BUCKETS // 2 - 1
FAR_POS_BUCKET = NUM_BUCKETS - 1

LANES = 128
BF16_ROWS = 16
F32_ROWS = 8
MXU_COLS = 256
PROJ_CHUNK = 2 * MXU_COLS
PROJ_CHUNK_ROWS = 256
ATT_TILE = 512
PROJ_ROWS = 1024
POST_ROWS = 512
ATT_PAIR = 2
VMEM_LIMIT = 56 * 1024 * 1024

F32 = jnp.float32
BF16 = jnp.bfloat16


def _dot(a, b):
    return jnp.dot(a, b, preferred_element_type=F32)


def _dot_nt(a, b):
    return lax.dot_general(a, b, (((1,), (1,)), ((), ())), preferred_element_type=F32)


def _rel_bucket(rel):
    nb = NUM_BUCKETS // 2
    ret = (rel > 0).astype(jnp.int32) * nb
    n = jnp.abs(rel)
    max_exact = nb // 2
    is_small = n < max_exact
    nf = jnp.maximum(n, 1).astype(F32)
    large = max_exact + (jnp.log(nf / max_exact) / math.log(MAX_DISTANCE / max_exact)
                         * (nb - max_exact)).astype(jnp.int32)
    large = jnp.minimum(large, nb - 1)
    return ret + jnp.where(is_small, n, large)


def _bias_kernel(rb_ref, out_ref):
    h = pl.program_id(0)
    t = ATT_TILE
    key = lax.broadcasted_iota(jnp.int32, (t, t), 0)
    qry = lax.broadcasted_iota(jnp.int32, (t, t), 1)
    for v in range(3):
        bucket = _rel_bucket(key - qry + (v - 1) * t)
        val = jnp.zeros((t, t), F32)
        for b in range(NUM_BUCKETS):
            val = jnp.where(bucket == b, rb_ref[b, h], val)
        far = rb_ref[FAR_NEG_BUCKET, h] if v == 0 else rb_ref[FAR_POS_BUCKET, h]
        out_ref[0, v] = (val - far) * LOG2E
    out_ref[0, 3] = jnp.zeros((t, t), F32)


def _bias_tiles(rel_bias):
    t = ATT_TILE
    return pl.pallas_call(
        _bias_kernel,
        out_shape=jax.ShapeDtypeStruct((N_HEADS, 4, t, t), F32),
        grid=(N_HEADS,),
        in_specs=[pl.BlockSpec(memory_space=pltpu.SMEM)],
        out_specs=pl.BlockSpec((1, 4, t, t), lambda h: (h, 0, 0, 0)),
        compiler_params=pltpu.CompilerParams(
            dimension_semantics=("arbitrary",), vmem_limit_bytes=VMEM_LIMIT),
        name="bias_tiles",
    )(rel_bias)


def _proj_kernel(x_ref, ng_ref, w_ref, qg_ref, kg_ref, o_ref, k2_ref, vt_ref, h_scr):
    n = pl.program_id(1)

    @pl.when(n == 0)
    def _():
        xf = x_ref[...]
        ms = jnp.mean(xf * xf, axis=-1, keepdims=True)
        h_scr[...] = (xf * lax.rsqrt(ms + EPS) * ng_ref[...]).astype(BF16)

    rows = h_scr.shape[0]

    def per_chunk(epilogue):
        for r0 in range(0, rows, PROJ_CHUNK_ROWS):
            rs = slice(r0, r0 + PROJ_CHUNK_ROWS)
            h = h_scr[rs, :]
            for c in range(D_MODEL // PROJ_CHUNK):
                acc = _dot(h, w_ref[:, c * PROJ_CHUNK:(c + 1) * PROJ_CHUNK])
                for j in range(PROJ_CHUNK // MXU_COLS):
                    lo = c * PROJ_CHUNK + j * MXU_COLS
                    epilogue(acc[:, j * MXU_COLS:(j + 1) * MXU_COLS], rs, slice(lo, lo + MXU_COLS))

    def qk_norm(acc, gain):
        gi = lax.broadcasted_iota(jnp.int32, (MXU_COLS, MXU_COLS), 0) // HEAD_DIM
        gj = lax.broadcasted_iota(jnp.int32, (MXU_COLS, MXU_COLS), 1) // HEAD_DIM
        ones_bd = (gi == gj).astype(BF16)
        ss = _dot((acc * acc).astype(BF16), ones_bd)
        return (acc * lax.rsqrt(ss * (1.0 / HEAD_DIM) + EPS) * gain).astype(BF16)

    @pl.when(n == 0)
    def _():
        def q_out(acc, rs, cs):
            o_ref[rs, cs] = qk_norm(acc, qg_ref[:, cs] * (SCALE * LOG2E))
        per_chunk(q_out)

    @pl.when(n == 1)
    def _():
        lane = lax.broadcasted_iota(jnp.int32, (PROJ_CHUNK_ROWS, MXU_COLS), 1) & (LANES - 1)
        aug1 = jnp.where((lane >> 1) == HEAD_DIM // 2, 1.0, 0.0).astype(BF16)
        aug2 = jnp.where(lane < 2, 1.0, 0.0).astype(BF16)

        def k_out(acc, rs, cs):
            kb = qk_norm(acc, kg_ref[:, cs])
            o_ref[rs, cs] = jnp.where(lane < HEAD_DIM, kb, aug1)
            k2_ref[rs, cs] = jnp.where(lane < HEAD_DIM, aug2, kb)
        per_chunk(k_out)

    @pl.when(n == 2)
    def _():
        def v_out(acc, rs, cs):
            o_ref[rs, cs] = acc.astype(BF16)
            vt_ref[cs, rs] = acc.T.astype(BF16)
        per_chunk(v_out)

    @pl.when((n == 3) | (n == 7))
    def _():
        def silu_out(acc, rs, cs):
            o_ref[rs, cs] = (acc * jax.nn.sigmoid(acc)).astype(BF16)
        per_chunk(silu_out)

    @pl.when((n >= 4) & (n <= 6))
    def _():
        def plain_out(acc, rs, cs):
            o_ref[rs, cs] = acc.astype(BF16)
        per_chunk(plain_out)

    @pl.when(n >= 8)
    def _():
        def sigmoid_out(acc, rs, cs):
            o_ref[rs, cs] = jax.nn.sigmoid(acc).astype(BF16)
        per_chunk(sigmoid_out)


def _proj(x2d, norm_g, w_in, qg, kg):
    t = x2d.shape[0]
    tm = PROJ_ROWS
    return pl.pallas_call(
        _proj_kernel,
        out_shape=(jax.ShapeDtypeStruct((t, N_COLS * D_MODEL), BF16),
                   jax.ShapeDtypeStruct((t, D_MODEL), BF16),
                   jax.ShapeDtypeStruct((D_MODEL, t), BF16)),
        grid=(t // tm, N_COLS),
        in_specs=[
            pl.BlockSpec((tm, D_MODEL), lambda m, n: (m, 0)),
            pl.BlockSpec((1, D_MODEL), lambda m, n: (0, 0)),
            pl.BlockSpec((D_MODEL, D_MODEL), lambda m, n: (0, n)),
            pl.BlockSpec((1, D_MODEL), lambda m, n: (0, 0)),
            pl.BlockSpec((1, D_MODEL), lambda m, n: (0, 0)),
        ],
        out_specs=(pl.BlockSpec((tm, D_MODEL), lambda m, n: (m, n)),
                   pl.BlockSpec((tm, D_MODEL), lambda m, n: (m, 0)),
                   pl.BlockSpec((D_MODEL, tm), lambda m, n: (0, m))),
        scratch_shapes=[pltpu.VMEM((tm, D_MODEL), BF16)],
        compiler_params=pltpu.CompilerParams(
            dimension_semantics=("arbitrary", "arbitrary"), vmem_limit_bytes=VMEM_LIMIT),
        name="proj",
    )(x2d, norm_g, w_in, qg, kg)


def _attn_kernel(rb_ref, q_ref, k1_ref, k2_ref, vt_ref, dl_ref, lq1_ref, lk1_ref, lq2_ref, lk2_ref,
                 sg_ref, o_ref, s_scr, p_scr, q_scr, m_scr, c_scr, *, lam_init, nq, n_pairs):
    t = ATT_TILE
    step = pl.program_id(0)

    @pl.when(step == 0)
    def _():
        s_scr[...] = jnp.zeros(s_scr.shape, F32)
        p_scr[...] = jnp.zeros(p_scr.shape, BF16)
        m_scr[...] = jnp.zeros(m_scr.shape, F32)
        c_scr[...] = jnp.zeros(c_scr.shape, F32)

    pair_a = jnp.minimum(step, n_pairs - 1)
    pair_c = jnp.maximum(step - 1, 0)
    ha = lax.rem(pair_a // (nq // ATT_PAIR), N_HEADS)
    lam = (jnp.exp(jnp.sum(lq1_ref[...] * lk1_ref[...], axis=-1, keepdims=True))
           - jnp.exp(jnp.sum(lq2_ref[...] * lk2_ref[...], axis=-1, keepdims=True))
           + lam_init)

    def slabs(x, rows):
        return x.reshape(x.shape[0] // rows, rows, x.shape[1])

    def key_tile(i, r):
        kt = i + r
        kt = jnp.where(kt >= nq, kt - nq, kt)
        return pl.ds(pl.multiple_of(kt * t, t), t)

    for u in range(ATT_PAIR):
        ia = lax.rem(pair_a * ATT_PAIR + u, nq)
        ic = lax.rem(pair_c * ATT_PAIR + u, nq)
        qrows = slice(u * t, (u + 1) * t)

        q = q_ref[qrows, :]
        lane = lax.broadcasted_iota(jnp.int32, (t, LANES), 1)
        for side, bucket in enumerate((FAR_NEG_BUCKET, FAR_POS_BUCKET)):
            cv = jnp.full((t, LANES), rb_ref[bucket, ha] * LOG2E, F32)
            hi = cv.astype(BF16).astype(F32)
            for comp in range(2):
                base = HEAD_DIM if comp == 0 else 0
                keep = (lane < HEAD_DIM) if comp == 0 else (lane >= HEAD_DIM)
                aug = jnp.where(lane == base, hi, jnp.where(lane == base + 1, cv - hi, 0.0))
                q_scr[u, comp, side] = jnp.where(keep, q, aug.astype(BF16))

        mb1 = m_scr[0]
        mb2 = m_scr[1]
        c1 = c_scr[0]
        rho = jnp.concatenate([c_scr[1], c_scr[1]], axis=0).astype(BF16)

        m1 = m2 = jnp.full((F32_ROWS, t), -jnp.inf, F32)
        l1 = l2 = jnp.zeros((BF16_ROWS, t), F32)
        acc = jnp.zeros((V_DIM, t), F32)
        for r in range(nq):
            a = slabs(p_scr[0, r], BF16_ROWS) - slabs(p_scr[1, r], BF16_ROWS) * rho
            acc = acc + _dot(vt_ref[:, key_tile(ic, r)], a.reshape(t, t))

            def probs(comp, mb):
                p = jnp.exp2((slabs(s_scr[comp, r], F32_ROWS) - mb).reshape(t, t).astype(BF16))
                p_scr[comp, r] = p
                parts = [p[i * BF16_ROWS:(i + 1) * BF16_ROWS, :] for i in range(t // BF16_ROWS)]
                while len(parts) > 1:
                    parts = [parts[i] + parts[i + 1] for i in range(0, len(parts), 2)]
                return parts[0].astype(F32)

            l1 = l1 + probs(0, mb1)
            l2 = l2 + probs(1, mb2)
            wrapped = ia + r >= nq
            side = 1 if r == 0 else jnp.where(wrapped, 0, 1)
            keys = key_tile(ia, r)
            s1 = _dot_nt(k1_ref[keys, :], q_scr[u, 0, side])
            s2 = _dot_nt(k2_ref[keys, :], q_scr[u, 1, side])
            if r in (0, 1, nq - 1):
                offset = (r + 1) % nq - 1
                dl = dl_ref[0, jnp.where(wrapped == (r == nq - 1), offset + 1, 3)]
                s1 = s1 + dl
                s2 = s2 + dl
            s_scr[0, r] = s1
            s_scr[1, r] = s2
            m1 = jnp.maximum(m1, jnp.max(slabs(s1, F32_ROWS), axis=0))
            m2 = jnp.maximum(m2, jnp.max(slabs(s2, F32_ROWS), axis=0))

        m_scr[0] = jnp.broadcast_to(jnp.max(m1, axis=0, keepdims=True), (F32_ROWS, t))
        m_scr[1] = jnp.broadcast_to(jnp.max(m2, axis=0, keepdims=True), (F32_ROWS, t))

        l1 = jnp.sum(l1, axis=0, keepdims=True)
        l2 = jnp.sum(l2, axis=0, keepdims=True)
        c_scr[0] = jnp.broadcast_to(1.0 / l1, (F32_ROWS, t))
        c_scr[1] = jnp.broadcast_to(lam * l1 / l2, (F32_ROWS, t))

        ot = (slabs(acc, F32_ROWS) * c1).reshape(V_DIM, t)
        ms = jnp.mean(ot * ot, axis=0, keepdims=True)
        ot = ot * lax.rsqrt(ms + EPS) * sg_ref[...] * (1.0 - lam_init)
        o_ref[qrows, :] = ot.T.astype(BF16)


def _attn(rel_bias, proj, k2a, vt, bias_tiles, lq1, lk1, lq2, lk2, subln_g, *, batch, seq, lam_init):
    t = ATT_TILE
    nq = seq // t
    assert nq >= 3 and nq % ATT_PAIR == 0
    ppq = nq // ATT_PAIR
    n_pairs = batch * N_HEADS * ppq

    def split(pair):
        bh = pair // ppq
        return bh // N_HEADS, lax.rem(bh, N_HEADS), lax.rem(pair, ppq)

    def a_pair(s):
        return split(jnp.minimum(s, n_pairs - 1))

    def c_pair(s):
        return split(jnp.maximum(s - 1, 0))

    def q_map(s):
        b, h, j = a_pair(s)
        return b * ppq + j, h

    def k1_map(s):
        b, h, _ = a_pair(s)
        return b, N_HEADS + h

    def k2_map(s):
        b, h, _ = a_pair(s)
        return b, h

    def v_map(s):
        b, h, _ = c_pair(s)
        return h, b

    def dl_map(s):
        return a_pair(s)[1], 0, 0, 0

    def o_map(s):
        b, h, j = c_pair(s)
        return b * ppq + j, h

    small = lambda shape: pl.BlockSpec(shape, lambda s: (0, 0))
    return pl.pallas_call(
        functools.partial(_attn_kernel, lam_init=lam_init, nq=nq, n_pairs=n_pairs),
        out_shape=jax.ShapeDtypeStruct((batch * seq, N_HEADS * V_DIM), BF16),
        grid=(n_pairs + 1,),
        in_specs=[
            pl.BlockSpec(memory_space=pltpu.SMEM),
            pl.BlockSpec((ATT_PAIR * t, LANES), q_map),
            pl.BlockSpec((seq, LANES), k1_map),
            pl.BlockSpec((seq, LANES), k2_map),
            pl.BlockSpec((V_DIM, seq), v_map),
            pl.BlockSpec((1, 4, t, t), dl_map),
            small((1, HEAD_DIM)), small((1, HEAD_DIM)), small((1, HEAD_DIM)), small((1, HEAD_DIM)),
            small((V_DIM, 1)),
        ],
        out_specs=pl.BlockSpec((ATT_PAIR * t, V_DIM), o_map),
        scratch_shapes=[
            pltpu.VMEM((2, nq, t, t), F32),
            pltpu.VMEM((2, nq, t, t), BF16),
            pltpu.VMEM((ATT_PAIR, 2, 2, t, LANES), BF16),
            pltpu.VMEM((2, F32_ROWS, t), F32),
            pltpu.VMEM((2, F32_ROWS, t), F32),
        ],
        compiler_params=pltpu.CompilerParams(
            dimension_semantics=("arbitrary",), vmem_limit_bytes=VMEM_LIMIT),
        name="attn",
    )(rel_bias, proj, proj, k2a, vt, bias_tiles, lq1, lk1, lq2, lk2, subln_g)


def _post_kernel(o_ref, sga_ref, xc_ref, cg_ref, bg_ref, sgb_ref, sma_ref, smb_ref,
                 xcp_ref, cgp_ref, xcn_ref, cgn_ref, x_ref, cw_ref, wa_ref, wc_ref, wo_ref,
                 out_ref, *, tiles_per_seq):
    tm = out_ref.shape[0]
    pos = lax.rem(pl.program_id(0), tiles_per_seq)

    ua = (o_ref[...].astype(F32) * sga_ref[...].astype(F32)).astype(BF16)
    ya = _dot(ua, wa_ref[...])

    pc = cg_ref[...].astype(F32) * xc_ref[...].astype(F32)
    prev_row = (cgp_ref[...].astype(F32) * xcp_ref[...].astype(F32))[BF16_ROWS - 1:BF16_ROWS, :]
    next_row = (cgn_ref[...].astype(F32) * xcn_ref[...].astype(F32))[0:1, :]
    prev_row = jnp.where(pos == 0, 0.0, prev_row)
    next_row = jnp.where(pos == tiles_per_seq - 1, 0.0, next_row)
    row = lax.broadcasted_iota(jnp.int32, pc.shape, 0)
    up = jnp.where(row == 0, prev_row, pltpu.roll(pc, 1, axis=0))
    dn = jnp.where(row == tm - 1, next_row, pltpu.roll(pc, tm - 1, axis=0))
    cw = cw_ref[...]
    cv = up * cw[0:1, :] + pc * cw[1:2, :] + dn * cw[2:3, :]
    ub = (bg_ref[...].astype(F32) * cv * sgb_ref[...].astype(F32)).astype(BF16)
    yb = _dot(ub, wc_ref[...])

    merged = (sma_ref[...].astype(F32) * ya + smb_ref[...].astype(F32) * yb).astype(BF16)
    out_ref[...] = x_ref[...] + _dot(merged, wo_ref[...])


def _post(o, proj, x2d, conv_w, wa, wc, wo, *, seq):
    t = x2d.shape[0]
    tm = POST_ROWS
    hb = tm // BF16_ROWS
    n_hb = t // BF16_ROWS
    col = lambda c: pl.BlockSpec((tm, D_MODEL), lambda m: (m, c))
    prev = lambda c: pl.BlockSpec((BF16_ROWS, D_MODEL), lambda m: (jnp.maximum(m * hb - 1, 0), c))
    nxt = lambda c: pl.BlockSpec((BF16_ROWS, D_MODEL),
                                 lambda m: (jnp.minimum((m + 1) * hb, n_hb - 1), c))
    whole = lambda r: pl.BlockSpec((r, D_MODEL), lambda m: (0, 0))
    return pl.pallas_call(
        functools.partial(_post_kernel, tiles_per_seq=seq // tm),
        out_shape=jax.ShapeDtypeStruct((t, D_MODEL), F32),
        grid=(t // tm,),
        in_specs=[
            col(0),
            col(3), col(4), col(5), col(6), col(7), col(8), col(9),
            prev(4), prev(5), nxt(4), nxt(5),
            col(0),
            whole(3), whole(D_MODEL), whole(D_MODEL), whole(D_MODEL),
        ],
        out_specs=pl.BlockSpec((tm, D_MODEL), lambda m: (m, 0)),
        compiler_params=pltpu.CompilerParams(
            dimension_semantics=("arbitrary",), vmem_limit_bytes=VMEM_LIMIT),
        name="post",
    )(o, proj, proj, proj, proj, proj, proj, proj, proj, proj, proj, proj, x2d,
      conv_w, wa, wc, wo)


def _lambda_init(layer_idx):
    return 0.8 - 0.6 * math.exp(-0.3 * layer_idx)


def _trunk(x, bias_tiles, p):
    batch, seq, _ = x.shape
    x2d = x.reshape(batch * seq, D_MODEL)
    for l in range(DEPTH):
        proj, k2a, vt = _proj(x2d, p["norm_g"][l], p["w_in"][l], p["qg"][l], p["kg"][l])
        o = _attn(p["rel_bias"], proj, k2a, vt, bias_tiles, p["lq1"][l], p["lk1"][l], p["lq2"][l],
                  p["lk2"][l], p["subln_g"][l], batch=batch, seq=seq, lam_init=_lambda_init(l))
        x2d = _post(o, proj, x2d, p["conv_w"][l], p["wa"][l], p["wc"][l], p["wo"][l], seq=seq)
    return x2d.reshape(batch, seq, D_MODEL)


def kernel(x_prompt, x_sample, rel_bias, norm_g, w_in, q_norm_g, k_norm_g, lambda_q1, lambda_k1,
           lambda_q2, lambda_k2, subln_g, w_attn_out, conv_w, w_conv_out, w_o):
    reps = D_MODEL // HEAD_DIM
    p = {
        "rel_bias": rel_bias,
        "norm_g": norm_g.reshape(DEPTH, 1, D_MODEL),
        "w_in": w_in.astype(BF16),
        "qg": jnp.tile(q_norm_g, (1, reps)).reshape(DEPTH, 1, D_MODEL),
        "kg": jnp.tile(k_norm_g, (1, reps)).reshape(DEPTH, 1, D_MODEL),
        "lq1": lambda_q1.reshape(DEPTH, 1, HEAD_DIM),
        "lk1": lambda_k1.reshape(DEPTH, 1, HEAD_DIM),
        "lq2": lambda_q2.reshape(DEPTH, 1, HEAD_DIM),
        "lk2": lambda_k2.reshape(DEPTH, 1, HEAD_DIM),
        "subln_g": subln_g.reshape(DEPTH, V_DIM, 1),
        "conv_w": conv_w,
        "wa": w_attn_out.astype(BF16),
        "wc": w_conv_out.astype(BF16),
        "wo": w_o.astype(BF16),
    }
    bias_tiles = _bias_tiles(rel_bias)
    return (_trunk(x_prompt, bias_tiles, p), _trunk(x_sample, bias_tiles, p))
```

```python
import functools
import math

import jax
import jax.numpy as jnp
from jax import lax
from jax.experimental import pallas as pl
from jax.experimental.pallas import tpu as pltpu

D_MODEL = 1024
DEPTH = 4
N_HEADS = 8
HEAD_DIM = 64
V_DIM = 2 * HEAD_DIM
SCALE = HEAD_DIM ** -0.5
NUM_BUCKETS = 32
MAX_DISTANCE = 128
N_COLS = 10
OUT_GROUPS = ("q", "k1", "silu_ga", "c_xc", "b_silu_gb", "sig_ma", "sig_mb")
EPS = 1e-6
LOG2E = 1.4426950408889634
FAR_NEG_BUCKET = NUM_BUCKETS // 2 - 1
FAR_POS_BUCKET = NUM_BUCKETS - 1

LANES = 128
BF16_ROWS = 16
F32_ROWS = 8
MXU_COLS = 256
PROJ_CHUNK = 2 * MXU_COLS
PROJ_CHUNK_ROWS = 512
ATT_TILE = 512
PROJ_ROWS = 1024
POST_ROWS = 512
ATT_PAIR = 2
VMEM_LIMIT = 56 * 1024 * 1024

F32 = jnp.float32
BF16 = jnp.bfloat16


def _dot(a, b):
    return jnp.dot(a, b, preferred_element_type=F32)


def _dot_nt(a, b):
    return lax.dot_general(a, b, (((1,), (1,)), ((), ())), preferred_element_type=F32)


def _rel_bucket(rel):
    nb = NUM_BUCKETS // 2
    ret = (rel > 0).astype(jnp.int32) * nb
    n = jnp.abs(rel)
    max_exact = nb // 2
    is_small = n < max_exact
    nf = jnp.maximum(n, 1).astype(F32)
    large = max_exact + (jnp.log(nf / max_exact) / math.log(MAX_DISTANCE / max_exact)
                         * (nb - max_exact)).astype(jnp.int32)
    large = jnp.minimum(large, nb - 1)
    return ret + jnp.where(is_small, n, large)


def _bias_kernel(rb_ref, out_ref):
    h = pl.program_id(0)
    t = ATT_TILE
    key = lax.broadcasted_iota(jnp.int32, (t, t), 0)
    qry = lax.broadcasted_iota(jnp.int32, (t, t), 1)
    for v in range(3):
        bucket = _rel_bucket(key - qry + (v - 1) * t)
        val = jnp.zeros((t, t), F32)
        for b in range(NUM_BUCKETS):
            val = jnp.where(bucket == b, rb_ref[b, h], val)
        far = rb_ref[FAR_NEG_BUCKET, h] if v == 0 else rb_ref[FAR_POS_BUCKET, h]
        out_ref[0, v] = (val - far) * LOG2E
    out_ref[0, 3] = jnp.zeros((t, t), F32)


def _bias_tiles(rel_bias):
    t = ATT_TILE
    return pl.pallas_call(
        _bias_kernel,
        out_shape=jax.ShapeDtypeStruct((N_HEADS, 4, t, t), F32),
        grid=(N_HEADS,),
        in_specs=[pl.BlockSpec(memory_space=pltpu.SMEM)],
        out_specs=pl.BlockSpec((1, 4, t, t), lambda h: (h, 0, 0, 0)),
        compiler_params=pltpu.CompilerParams(
            dimension_semantics=("arbitrary",), vmem_limit_bytes=VMEM_LIMIT),
        name="bias_tiles",
    )(rel_bias)


def _proj_kernel(x_ref, ng_ref, w_ref, qg_ref, kg_ref, o_ref, k2_ref, vt_ref, h_scr, keep_scr):
    n = pl.program_id(1)

    @pl.when(n == 0)
    def _():
        xf = x_ref[...]
        ms = jnp.mean(xf * xf, axis=-1, keepdims=True)
        h_scr[...] = (xf * lax.rsqrt(ms + EPS) * ng_ref[...]).astype(BF16)

    rows = h_scr.shape[0]

    def per_chunk(epilogue):
        for r0 in range(0, rows, PROJ_CHUNK_ROWS):
            rs = slice(r0, r0 + PROJ_CHUNK_ROWS)
            h = h_scr[rs, :]
            for c in range(D_MODEL // PROJ_CHUNK):
                acc = _dot(h, w_ref[:, c * PROJ_CHUNK:(c + 1) * PROJ_CHUNK])
                for j in range(PROJ_CHUNK // MXU_COLS):
                    lo = c * PROJ_CHUNK + j * MXU_COLS
                    epilogue(acc[:, j * MXU_COLS:(j + 1) * MXU_COLS], rs, slice(lo, lo + MXU_COLS))

    def qk_norm(acc, gain):
        gi = lax.broadcasted_iota(jnp.int32, (MXU_COLS, MXU_COLS), 0) // HEAD_DIM
        gj = lax.broadcasted_iota(jnp.int32, (MXU_COLS, MXU_COLS), 1) // HEAD_DIM
        ones_bd = (gi == gj).astype(BF16)
        ss = _dot((acc * acc).astype(BF16), ones_bd)
        return (acc * lax.rsqrt(ss * (1.0 / HEAD_DIM) + EPS) * gain).astype(BF16)

    @pl.when(n == 0)
    def _():
        def q_out(acc, rs, cs):
            o_ref[rs, cs] = qk_norm(acc, qg_ref[:, cs] * (SCALE * LOG2E))
        per_chunk(q_out)

    @pl.when(n == 1)
    def _():
        lane = lax.broadcasted_iota(jnp.int32, (PROJ_CHUNK_ROWS, MXU_COLS), 1) & (LANES - 1)
        aug1 = jnp.where((lane >> 1) == HEAD_DIM // 2, 1.0, 0.0).astype(BF16)
        aug2 = jnp.where(lane < 2, 1.0, 0.0).astype(BF16)

        def k_out(acc, rs, cs):
            kb = qk_norm(acc, kg_ref[:, cs])
            o_ref[rs, cs] = jnp.where(lane < HEAD_DIM, kb, aug1)
            k2_ref[rs, cs] = jnp.where(lane < HEAD_DIM, aug2, kb)
        per_chunk(k_out)

    @pl.when(n == 2)
    def _():
        def v_out(acc, rs, cs):
            vt_ref[cs, rs] = acc.T.astype(BF16)
        per_chunk(v_out)

    @pl.when(n == 3)
    def _():
        def silu_out(acc, rs, cs):
            o_ref[rs, cs] = (acc * jax.nn.sigmoid(acc)).astype(BF16)
        per_chunk(silu_out)

    @pl.when((n == 4) | (n == 6))
    def _():
        def keep(acc, rs, cs):
            keep_scr[rs, cs] = acc.astype(BF16)
        per_chunk(keep)

    @pl.when(n == 5)
    def _():
        def conv_in(acc, rs, cs):
            o_ref[rs, cs] = (acc * keep_scr[rs, cs].astype(F32)).astype(BF16)
        per_chunk(conv_in)

    @pl.when(n == 7)
    def _():
        def gate_out(acc, rs, cs):
            o_ref[rs, cs] = (keep_scr[rs, cs].astype(F32) * (acc * jax.nn.sigmoid(acc))).astype(BF16)
        per_chunk(gate_out)

    @pl.when(n >= 8)
    def _():
        def sigmoid_out(acc, rs, cs):
            o_ref[rs, cs] = jax.nn.sigmoid(acc).astype(BF16)
        per_chunk(sigmoid_out)


def _proj(x2d, norm_g, w_in, qg, kg):
    t = x2d.shape[0]
    tm = PROJ_ROWS

    def out_map(m, n):
        skipped = (n >= 3).astype(jnp.int32) + (n >= 5).astype(jnp.int32) + (n >= 7).astype(jnp.int32)
        return m, n - skipped

    return pl.pallas_call(
        _proj_kernel,
        out_shape=(jax.ShapeDtypeStruct((t, len(OUT_GROUPS) * D_MODEL), BF16),
                   jax.ShapeDtypeStruct((t, D_MODEL), BF16),
                   jax.ShapeDtypeStruct((D_MODEL, t), BF16)),
        grid=(t // tm, N_COLS),
        in_specs=[
            pl.BlockSpec((tm, D_MODEL), lambda m, n: (m, 0)),
            pl.BlockSpec((1, D_MODEL), lambda m, n: (0, 0)),
            pl.BlockSpec((D_MODEL, D_MODEL), lambda m, n: (0, n)),
            pl.BlockSpec((1, D_MODEL), lambda m, n: (0, 0)),
            pl.BlockSpec((1, D_MODEL), lambda m, n: (0, 0)),
        ],
        out_specs=(pl.BlockSpec((tm, D_MODEL), out_map),
                   pl.BlockSpec((tm, D_MODEL), lambda m, n: (m, 0)),
                   pl.BlockSpec((D_MODEL, tm), lambda m, n: (0, m))),
        scratch_shapes=[pltpu.VMEM((tm, D_MODEL), BF16), pltpu.VMEM((tm, D_MODEL), BF16)],
        compiler_params=pltpu.CompilerParams(
            dimension_semantics=("arbitrary", "arbitrary"), vmem_limit_bytes=VMEM_LIMIT),
        name="proj",
    )(x2d, norm_g, w_in, qg, kg)


def _attn_kernel(rb_ref, q_ref, k1_ref, k2_ref, vt_ref, dl_ref, lq1_ref, lk1_ref, lq2_ref, lk2_ref,
                 sg_ref, o_ref, s_scr, p_scr, q_scr, m_scr, c_scr, *, lam_init, nq, n_pairs):
    t = ATT_TILE
    step = pl.program_id(0)

    @pl.when(step == 0)
    def _():
        s_scr[...] = jnp.zeros(s_scr.shape, BF16)
        p_scr[...] = jnp.zeros(p_scr.shape, BF16)
        m_scr[...] = jnp.zeros(m_scr.shape, BF16)
        c_scr[...] = jnp.zeros(c_scr.shape, F32)

    pair_a = jnp.minimum(step, n_pairs - 1)
    pair_c = jnp.maximum(step - 1, 0)
    ha = lax.rem(pair_a // (nq // ATT_PAIR), N_HEADS)
    lam = (jnp.exp(jnp.sum(lq1_ref[...] * lk1_ref[...], axis=-1, keepdims=True))
           - jnp.exp(jnp.sum(lq2_ref[...] * lk2_ref[...], axis=-1, keepdims=True))
           + lam_init)

    def slabs(x, rows):
        return x.reshape(x.shape[0] // rows, rows, x.shape[1])

    def key_tile(i, r):
        kt = i + r
        kt = jnp.where(kt >= nq, kt - nq, kt)
        return pl.ds(pl.multiple_of(kt * t, t), t)

    for u in range(ATT_PAIR):
        ia = lax.rem(pair_a * ATT_PAIR + u, nq)
        ic = lax.rem(pair_c * ATT_PAIR + u, nq)
        qrows = slice(u * t, (u + 1) * t)

        q = q_ref[qrows, :]
        lane = lax.broadcasted_iota(jnp.int32, (t, LANES), 1)
        for side, bucket in enumerate((FAR_NEG_BUCKET, FAR_POS_BUCKET)):
            cv = jnp.full((t, LANES), rb_ref[bucket, ha] * LOG2E, F32)
            hi = cv.astype(BF16).astype(F32)
            for comp in range(2):
                base = HEAD_DIM if comp == 0 else 0
                keep = (lane < HEAD_DIM) if comp == 0 else (lane >= HEAD_DIM)
                aug = jnp.where(lane == base, hi, jnp.where(lane == base + 1, cv - hi, 0.0))
                q_scr[u, comp, side] = jnp.where(keep, q, aug.astype(BF16))

        buf_a, buf_b = u % 2, (u + 1) % 2
        mb1 = m_scr[buf_b, 0]
        mb2 = m_scr[buf_b, 1]
        c1 = c_scr[0]
        rho = jnp.concatenate([c_scr[1], c_scr[1]], axis=0).astype(BF16)

        m1 = m2 = jnp.full((F32_ROWS, t), -jnp.inf, F32)
        l1 = l2 = jnp.zeros((BF16_ROWS, t), F32)
        acc = jnp.zeros((V_DIM, t), F32)
        for r in range(nq):
            a = slabs(p_scr[0, r], BF16_ROWS) - slabs(p_scr[1, r], BF16_ROWS) * rho
            acc = acc + _dot(vt_ref[:, key_tile(ic, r)], a.reshape(t, t))

            def probs(comp, mb):
                p = jnp.exp2((slabs(s_scr[buf_b, comp, r], BF16_ROWS) - mb).reshape(t, t))
                p_scr[comp, r] = p
                parts = [p[i * BF16_ROWS:(i + 1) * BF16_ROWS, :] for i in range(t // BF16_ROWS)]
                while len(parts) > 1:
                    parts = [parts[i] + parts[i + 1] for i in range(0, len(parts), 2)]
                return parts[0].astype(F32)

            l1 = l1 + probs(0, mb1)
            l2 = l2 + probs(1, mb2)
            wrapped = ia + r >= nq
            side = 1 if r == 0 else jnp.where(wrapped, 0, 1)
            keys = key_tile(ia, r)
            s1 = _dot_nt(k1_ref[keys, :], q_scr[u, 0, side])
            s2 = _dot_nt(k2_ref[keys, :], q_scr[u, 1, side])
            if r in (0, 1, nq - 1):
                offset = (r + 1) % nq - 1
                dl = dl_ref[0, jnp.where(wrapped == (r == nq - 1), offset + 1, 3)]
                s1 = s1 + dl
                s2 = s2 + dl
            s_scr[buf_a, 0, r] = s1.astype(BF16)
            s_scr[buf_a, 1, r] = s2.astype(BF16)
            m1 = jnp.maximum(m1, jnp.max(slabs(s1, F32_ROWS), axis=0))
            m2 = jnp.maximum(m2, jnp.max(slabs(s2, F32_ROWS), axis=0))

        for comp, m in enumerate((m1, m2)):
            m_scr[buf_a, comp] = jnp.broadcast_to(
                jnp.max(m, axis=0, keepdims=True), (BF16_ROWS, t)).astype(BF16)

        l1 = jnp.sum(l1, axis=0, keepdims=True)
        l2 = jnp.sum(l2, axis=0, keepdims=True)
        c_scr[0] = jnp.broadcast_to(1.0 / l1, (F32_ROWS, t))
        c_scr[1] = jnp.broadcast_to(lam * l1 / l2, (F32_ROWS, t))

        ot = (slabs(acc, F32_ROWS) * c1).reshape(V_DIM, t)
        ms = jnp.mean(ot * ot, axis=0, keepdims=True)
        ot = ot * lax.rsqrt(ms + EPS) * sg_ref[...] * (1.0 - lam_init)
        o_ref[qrows, :] = ot.T.astype(BF16)


def _attn(rel_bias, proj, k2a, vt, bias_tiles, lq1, lk1, lq2, lk2, subln_g, *, batch, seq, lam_init):
    t = ATT_TILE
    nq = seq // t
    assert nq >= 3 and nq % ATT_PAIR == 0
    ppq = nq // ATT_PAIR
    n_pairs = batch * N_HEADS * ppq

    def split(pair):
        bh = pair // ppq
        return bh // N_HEADS, lax.rem(bh, N_HEADS), lax.rem(pair, ppq)

    def a_pair(s):
        return split(jnp.minimum(s, n_pairs - 1))

    def c_pair(s):
        return split(jnp.maximum(s - 1, 0))

    def q_map(s):
        b, h, j = a_pair(s)
        return b * ppq + j, h

    def k1_map(s):
        b, h, _ = a_pair(s)
        return b, N_HEADS + h

    def k2_map(s):
        b, h, _ = a_pair(s)
        return b, h

    def v_map(s):
        b, h, _ = c_pair(s)
        return h, b

    def dl_map(s):
        return a_pair(s)[1], 0, 0, 0

    def o_map(s):
        b, h, j = c_pair(s)
        return b * ppq + j, h

    small = lambda shape: pl.BlockSpec(shape, lambda s: (0, 0))
    return pl.pallas_call(
        functools.partial(_attn_kernel, lam_init=lam_init, nq=nq, n_pairs=n_pairs),
        out_shape=jax.ShapeDtypeStruct((batch * seq, N_HEADS * V_DIM), BF16),
        grid=(n_pairs + 1,),
        in_specs=[
            pl.BlockSpec(memory_space=pltpu.SMEM),
            pl.BlockSpec((ATT_PAIR * t, LANES), q_map),
            pl.BlockSpec((seq, LANES), k1_map),
            pl.BlockSpec((seq, LANES), k2_map),
            pl.BlockSpec((V_DIM, seq), v_map),
            pl.BlockSpec((1, 4, t, t), dl_map),
            small((1, HEAD_DIM)), small((1, HEAD_DIM)), small((1, HEAD_DIM)), small((1, HEAD_DIM)),
            small((V_DIM, 1)),
        ],
        out_specs=pl.BlockSpec((ATT_PAIR * t, V_DIM), o_map),
        scratch_shapes=[
            pltpu.VMEM((2, 2, nq, t, t), BF16),
            pltpu.VMEM((2, nq, t, t), BF16),
            pltpu.VMEM((ATT_PAIR, 2, 2, t, LANES), BF16),
            pltpu.VMEM((2, 2, BF16_ROWS, t), BF16),
            pltpu.VMEM((2, F32_ROWS, t), F32),
        ],
        compiler_params=pltpu.CompilerParams(
            dimension_semantics=("arbitrary",), vmem_limit_bytes=VMEM_LIMIT),
        name="attn",
    )(rel_bias, proj, proj, k2a, vt, bias_tiles, lq1, lk1, lq2, lk2, subln_g)


def _post_kernel(o_ref, sga_ref, pc_ref, bg_ref, sma_ref, smb_ref, pcp_ref, pcn_ref, x_ref, cw_ref,
                 wa_ref, wc_ref, wo_ref, out_ref, *, tiles_per_seq):
    tm = out_ref.shape[0]
    pos = lax.rem(pl.program_id(0), tiles_per_seq)

    ua = (o_ref[...].astype(F32) * sga_ref[...].astype(F32)).astype(BF16)
    ya = _dot(ua, wa_ref[...])

    pc = pc_ref[...].astype(F32)
    prev_row = pcp_ref[...].astype(F32)[BF16_ROWS - 1:BF16_ROWS, :]
    next_row = pcn_ref[...].astype(F32)[0:1, :]
    prev_row = jnp.where(pos == 0, 0.0, prev_row)
    next_row = jnp.where(pos == tiles_per_seq - 1, 0.0, next_row)
    row = lax.broadcasted_iota(jnp.int32, pc.shape, 0)
    up = jnp.where(row == 0, prev_row, pltpu.roll(pc, 1, axis=0))
    dn = jnp.where(row == tm - 1, next_row, pltpu.roll(pc, tm - 1, axis=0))
    cw = cw_ref[...]
    cv = up * cw[0:1, :] + pc * cw[1:2, :] + dn * cw[2:3, :]
    ub = (bg_ref[...].astype(F32) * cv).astype(BF16)
    yb = _dot(ub, wc_ref[...])

    merged = (sma_ref[...].astype(F32) * ya + smb_ref[...].astype(F32) * yb).astype(BF16)
    out_ref[...] = x_ref[...] + _dot(merged, wo_ref[...])


def _post(o, proj, x2d, conv_w, wa, wc, wo, *, seq):
    t = x2d.shape[0]
    tm = POST_ROWS
    hb = tm // BF16_ROWS
    n_hb = t // BF16_ROWS
    grp = OUT_GROUPS.index
    col = lambda c: pl.BlockSpec((tm, D_MODEL), lambda m: (m, c))
    prev = lambda c: pl.BlockSpec((BF16_ROWS, D_MODEL), lambda m: (jnp.maximum(m * hb - 1, 0), c))
    nxt = lambda c: pl.BlockSpec((BF16_ROWS, D_MODEL),
                                 lambda m: (jnp.minimum((m + 1) * hb, n_hb - 1), c))
    whole = lambda r: pl.BlockSpec((r, D_MODEL), lambda m: (0, 0))
    return pl.pallas_call(
        functools.partial(_post_kernel, tiles_per_seq=seq // tm),
        out_shape=jax.ShapeDtypeStruct((t, D_MODEL), F32),
        grid=(t // tm,),
        in_specs=[
            col(0),
            col(grp("silu_ga")), col(grp("c_xc")), col(grp("b_silu_gb")),
            col(grp("sig_ma")), col(grp("sig_mb")),
            prev(grp("c_xc")), nxt(grp("c_xc")),
            col(0),
            whole(3), whole(D_MODEL), whole(D_MODEL), whole(D_MODEL),
        ],
        out_specs=pl.BlockSpec((tm, D_MODEL), lambda m: (m, 0)),
        compiler_params=pltpu.CompilerParams(
            dimension_semantics=("arbitrary",), vmem_limit_bytes=VMEM_LIMIT),
        name="post",
    )(o, proj, proj, proj, proj, proj, proj, proj, x2d, conv_w, wa, wc, wo)


def _lambda_init(layer_idx):
    return 0.8 - 0.6 * math.exp(-0.3 * layer_idx)


def _trunk(x, bias_tiles, p):
    batch, seq, _ = x.shape
    x2d = x.reshape(batch * seq, D_MODEL)
    for l in range(DEPTH):
        proj, k2a, vt = _proj(x2d, p["norm_g"][l], p["w_in"][l], p["qg"][l], p["kg"][l])
        o = _attn(p["rel_bias"], proj, k2a, vt, bias_tiles, p["lq1"][l], p["lk1"][l], p["lq2"][l],
                  p["lk2"][l], p["subln_g"][l], batch=batch, seq=seq, lam_init=_lambda_init(l))
        x2d = _post(o, proj, x2d, p["conv_w"][l], p["wa"][l], p["wc"][l], p["wo"][l], seq=seq)
    return x2d.reshape(batch, seq, D_MODEL)


def kernel(x_prompt, x_sample, rel_bias, norm_g, w_in, q_norm_g, k_norm_g, lambda_q1, lambda_k1,
           lambda_q2, lambda_k2, subln_g, w_attn_out, conv_w, w_conv_out, w_o):
    reps = D_MODEL // HEAD_DIM
    p = {
        "rel_bias": rel_bias,
        "norm_g": norm_g.reshape(DEPTH, 1, D_MODEL),
        "w_in": w_in.astype(BF16),
        "qg": jnp.tile(q_norm_g, (1, reps)).reshape(DEPTH, 1, D_MODEL),
        "kg": jnp.tile(k_norm_g, (1, reps)).reshape(DEPTH, 1, D_MODEL),
        "lq1": lambda_q1.reshape(DEPTH, 1, HEAD_DIM),
        "lk1": lambda_k1.reshape(DEPTH, 1, HEAD_DIM),
        "lq2": lambda_q2.reshape(DEPTH, 1, HEAD_DIM),
        "lk2": lambda_k2.reshape(DEPTH, 1, HEAD_DIM),
        "subln_g": subln_g.reshape(DEPTH, V_DIM, 1),
        "conv_w": conv_w,
        "wa": w_attn_out.astype(BF16),
        "wc": w_conv_out.astype(BF16),
        "wo": w_o.astype(BF16),
    }
    bias_tiles = _bias_tiles(rel_bias)
    return (_trunk(x_prompt, bias_tiles, p), _trunk(x_sample, bias_tiles, p))
```

```python
import functools
import math

import jax
import jax.numpy as jnp
from jax import lax
from jax.experimental import pallas as pl
from jax.experimental.pallas import tpu as pltpu

D_MODEL = 1024
DEPTH = 4
N_HEADS = 8
HEAD_DIM = 64
V_DIM = 2 * HEAD_DIM
SCALE = HEAD_DIM ** -0.5
NUM_BUCKETS = 32
MAX_DISTANCE = 128
N_COLS = 10
OUT_GROUPS = ("q", "k1", "silu_ga", "c_xc", "b_silu_gb", "sig_ma", "sig_mb")
EPS = 1e-6
LOG2E = 1.4426950408889634
FAR_NEG_BUCKET = NUM_BUCKETS // 2 - 1
FAR_POS_BUCKET = NUM_BUCKETS - 1

LANES = 128
BF16_ROWS = 16
F32_ROWS = 8
MXU_COLS = 256
PROJ_CHUNK = 2 * MXU_COLS
PROJ_CHUNK_ROWS = 512
ATT_TILE = 512
PROJ_ROWS = 1024
POST_ROWS = 512
ATT_PAIR = 2
VMEM_LIMIT = 56 * 1024 * 1024

F32 = jnp.float32
BF16 = jnp.bfloat16


def _dot(a, b):
    return jnp.dot(a, b, preferred_element_type=F32)


def _dot_nt(a, b):
    return lax.dot_general(a, b, (((1,), (1,)), ((), ())), preferred_element_type=F32)


def _rel_bucket(rel):
    nb = NUM_BUCKETS // 2
    ret = (rel > 0).astype(jnp.int32) * nb
    n = jnp.abs(rel)
    max_exact = nb // 2
    is_small = n < max_exact
    nf = jnp.maximum(n, 1).astype(F32)
    large = max_exact + (jnp.log(nf / max_exact) / math.log(MAX_DISTANCE / max_exact)
                         * (nb - max_exact)).astype(jnp.int32)
    large = jnp.minimum(large, nb - 1)
    return ret + jnp.where(is_small, n, large)


def _bias_kernel(rb_ref, out_ref):
    h = pl.program_id(0)
    t = ATT_TILE
    key = lax.broadcasted_iota(jnp.int32, (t, t), 0)
    qry = lax.broadcasted_iota(jnp.int32, (t, t), 1)
    for v in range(3):
        bucket = _rel_bucket(key - qry + (v - 1) * t)
        val = jnp.zeros((t, t), F32)
        for b in range(NUM_BUCKETS):
            val = jnp.where(bucket == b, rb_ref[b, h], val)
        far = rb_ref[FAR_NEG_BUCKET, h] if v == 0 else rb_ref[FAR_POS_BUCKET, h]
        out_ref[0, v] = (val - far) * LOG2E
    out_ref[0, 3] = jnp.zeros((t, t), F32)


def _bias_tiles(rel_bias):
    t = ATT_TILE
    return pl.pallas_call(
        _bias_kernel,
        out_shape=jax.ShapeDtypeStruct((N_HEADS, 4, t, t), F32),
        grid=(N_HEADS,),
        in_specs=[pl.BlockSpec(memory_space=pltpu.SMEM)],
        out_specs=pl.BlockSpec((1, 4, t, t), lambda h: (h, 0, 0, 0)),
        compiler_params=pltpu.CompilerParams(
            dimension_semantics=("arbitrary",), vmem_limit_bytes=VMEM_LIMIT),
        name="bias_tiles",
    )(rel_bias)


def _proj_kernel(x_ref, ng_ref, w_ref, qg_ref, kg_ref, o_ref, k2_ref, vt_ref, h_scr, keep_scr):
    n = pl.program_id(1)

    @pl.when(n == 0)
    def _():
        xf = x_ref[...]
        ms = jnp.mean(xf * xf, axis=-1, keepdims=True)
        h_scr[...] = (xf * lax.rsqrt(ms + EPS) * ng_ref[...]).astype(BF16)

    rows = h_scr.shape[0]

    def per_chunk(epilogue):
        for r0 in range(0, rows, PROJ_CHUNK_ROWS):
            rs = slice(r0, r0 + PROJ_CHUNK_ROWS)
            h = h_scr[rs, :]
            for c in range(D_MODEL // PROJ_CHUNK):
                acc = _dot(h, w_ref[:, c * PROJ_CHUNK:(c + 1) * PROJ_CHUNK])
                for j in range(PROJ_CHUNK // MXU_COLS):
                    lo = c * PROJ_CHUNK + j * MXU_COLS
                    epilogue(acc[:, j * MXU_COLS:(j + 1) * MXU_COLS], rs, slice(lo, lo + MXU_COLS))

    def qk_norm(acc, gain):
        gi = lax.broadcasted_iota(jnp.int32, (MXU_COLS, MXU_COLS), 0) // HEAD_DIM
        gj = lax.broadcasted_iota(jnp.int32, (MXU_COLS, MXU_COLS), 1) // HEAD_DIM
        ones_bd = (gi == gj).astype(BF16)
        ss = _dot((acc * acc).astype(BF16), ones_bd)
        return (acc * lax.rsqrt(ss * (1.0 / HEAD_DIM) + EPS) * gain).astype(BF16)

    @pl.when(n == 0)
    def _():
        def q_out(acc, rs, cs):
            o_ref[rs, cs] = qk_norm(acc, qg_ref[:, cs] * (SCALE * LOG2E))
        per_chunk(q_out)

    @pl.when(n == 1)
    def _():
        lane = lax.broadcasted_iota(jnp.int32, (PROJ_CHUNK_ROWS, MXU_COLS), 1) & (LANES - 1)
        aug1 = jnp.where((lane >> 1) == HEAD_DIM // 2, 1.0, 0.0).astype(BF16)
        aug2 = jnp.where(lane < 2, 1.0, 0.0).astype(BF16)

        def k_out(acc, rs, cs):
            kb = qk_norm(acc, kg_ref[:, cs])
            o_ref[rs, cs] = jnp.where(lane < HEAD_DIM, kb, aug1)
            k2_ref[rs, cs] = jnp.where(lane < HEAD_DIM, aug2, kb)
        per_chunk(k_out)

    @pl.when(n == 2)
    def _():
        def v_out(acc, rs, cs):
            vt_ref[cs, rs] = acc.T.astype(BF16)
        per_chunk(v_out)

    @pl.when(n == 3)
    def _():
        def silu_out(acc, rs, cs):
            o_ref[rs, cs] = (acc * jax.nn.sigmoid(acc)).astype(BF16)
        per_chunk(silu_out)

    @pl.when((n == 4) | (n == 6))
    def _():
        def keep(acc, rs, cs):
            keep_scr[rs, cs] = acc.astype(BF16)
        per_chunk(keep)

    @pl.when(n == 5)
    def _():
        def conv_in(acc, rs, cs):
            o_ref[rs, cs] = (acc * keep_scr[rs, cs].astype(F32)).astype(BF16)
        per_chunk(conv_in)

    @pl.when(n == 7)
    def _():
        def gate_out(acc, rs, cs):
            o_ref[rs, cs] = (keep_scr[rs, cs].astype(F32) * (acc * jax.nn.sigmoid(acc))).astype(BF16)
        per_chunk(gate_out)

    @pl.when(n >= 8)
    def _():
        def sigmoid_out(acc, rs, cs):
            o_ref[rs, cs] = jax.nn.sigmoid(acc).astype(BF16)
        per_chunk(sigmoid_out)


def _proj(x2d, norm_g, w_in, qg, kg):
    t = x2d.shape[0]
    tm = PROJ_ROWS

    def out_map(m, n):
        skipped = (n >= 3).astype(jnp.int32) + (n >= 5).astype(jnp.int32) + (n >= 7).astype(jnp.int32)
        return m, n - skipped

    return pl.pallas_call(
        _proj_kernel,
        out_shape=(jax.ShapeDtypeStruct((t, len(OUT_GROUPS) * D_MODEL), BF16),
                   jax.ShapeDtypeStruct((t, D_MODEL), BF16),
                   jax.ShapeDtypeStruct((D_MODEL, t), BF16)),
        grid=(t // tm, N_COLS),
        in_specs=[
            pl.BlockSpec((tm, D_MODEL), lambda m, n: (m, 0)),
            pl.BlockSpec((1, D_MODEL), lambda m, n: (0, 0)),
            pl.BlockSpec((D_MODEL, D_MODEL), lambda m, n: (0, n)),
            pl.BlockSpec((1, D_MODEL), lambda m, n: (0, 0)),
            pl.BlockSpec((1, D_MODEL), lambda m, n: (0, 0)),
        ],
        out_specs=(pl.BlockSpec((tm, D_MODEL), out_map),
                   pl.BlockSpec((tm, D_MODEL), lambda m, n: (m, 0)),
                   pl.BlockSpec((D_MODEL, tm), lambda m, n: (0, m))),
        scratch_shapes=[pltpu.VMEM((tm, D_MODEL), BF16), pltpu.VMEM((tm, D_MODEL), BF16)],
        compiler_params=pltpu.CompilerParams(
            dimension_semantics=("arbitrary", "arbitrary"), vmem_limit_bytes=VMEM_LIMIT),
        name="proj",
    )(x2d, norm_g, w_in, qg, kg)


def _attn_kernel(rb_ref, q_ref, k1_ref, k2_ref, vt_ref, dl_ref, lq1_ref, lk1_ref, lq2_ref, lk2_ref,
                 sg_ref, o_ref, s_scr, p_scr, q_scr, m_scr, c_scr, *, lam_init, nq, n_pairs):
    t = ATT_TILE
    step = pl.program_id(0)

    @pl.when(step == 0)
    def _():
        s_scr[...] = jnp.zeros(s_scr.shape, BF16)
        p_scr[...] = jnp.zeros(p_scr.shape, BF16)
        m_scr[...] = jnp.zeros(m_scr.shape, BF16)
        c_scr[...] = jnp.zeros(c_scr.shape, F32)

    pair_a = jnp.minimum(step, n_pairs - 1)
    pair_c = jnp.maximum(step - 1, 0)
    ha = lax.rem(pair_a // (nq // ATT_PAIR), N_HEADS)
    lam = (jnp.exp(jnp.sum(lq1_ref[...] * lk1_ref[...], axis=-1, keepdims=True))
           - jnp.exp(jnp.sum(lq2_ref[...] * lk2_ref[...], axis=-1, keepdims=True))
           + lam_init)

    def slabs(x, rows):
        return x.reshape(x.shape[0] // rows, rows, x.shape[1])

    def key_tile(i, r):
        kt = i + r
        kt = jnp.where(kt >= nq, kt - nq, kt)
        return pl.ds(pl.multiple_of(kt * t, t), t)

    for u in range(ATT_PAIR):
        ia = lax.rem(pair_a * ATT_PAIR + u, nq)
        ic = lax.rem(pair_c * ATT_PAIR + u, nq)
        qrows = slice(u * t, (u + 1) * t)

        q = q_ref[qrows, :]
        lane = lax.broadcasted_iota(jnp.int32, (t, LANES), 1)
        for side, bucket in enumerate((FAR_NEG_BUCKET, FAR_POS_BUCKET)):
            cv = jnp.full((t, LANES), rb_ref[bucket, ha] * LOG2E, F32)
            hi = cv.astype(BF16).astype(F32)
            for comp in range(2):
                base = HEAD_DIM if comp == 0 else 0
                keep = (lane < HEAD_DIM) if comp == 0 else (lane >= HEAD_DIM)
                aug = jnp.where(lane == base, hi, jnp.where(lane == base + 1, cv - hi, 0.0))
                q_scr[u, comp, side] = jnp.where(keep, q, aug.astype(BF16))

        buf_a, buf_b = u % 2, (u + 1) % 2
        mb1 = m_scr[buf_b, 0]
        mb2 = m_scr[buf_b, 1]
        c1 = c_scr[0]
        rho = jnp.concatenate([c_scr[1], c_scr[1]], axis=0).astype(BF16)

        m1 = m2 = jnp.full((BF16_ROWS, t), -jnp.inf, BF16)
        l1 = l2 = jnp.zeros((BF16_ROWS, t), F32)
        acc = jnp.zeros((V_DIM, t), F32)
        for r in range(nq):
            a = slabs(p_scr[0, r], BF16_ROWS) - slabs(p_scr[1, r], BF16_ROWS) * rho
            acc = acc + _dot(vt_ref[:, key_tile(ic, r)], a.reshape(t, t))

            def probs(comp, mb):
                p = jnp.exp2((slabs(s_scr[buf_b, comp, r], BF16_ROWS) - mb).reshape(t, t))
                p_scr[comp, r] = p
                parts = [p[i * BF16_ROWS:(i + 1) * BF16_ROWS, :] for i in range(t // BF16_ROWS)]
                while len(parts) > 1:
                    parts = [parts[i] + parts[i + 1] for i in range(0, len(parts), 2)]
                return parts[0].astype(F32)

            l1 = l1 + probs(0, mb1)
            l2 = l2 + probs(1, mb2)
            wrapped = ia + r >= nq
            side = 1 if r == 0 else jnp.where(wrapped, 0, 1)
            keys = key_tile(ia, r)
            s1 = _dot_nt(k1_ref[keys, :], q_scr[u, 0, side])
            s2 = _dot_nt(k2_ref[keys, :], q_scr[u, 1, side])
            if r in (0, 1, nq - 1):
                offset = (r + 1) % nq - 1
                dl = dl_ref[0, jnp.where(wrapped == (r == nq - 1), offset + 1, 3)]
                s1 = s1 + dl
                s2 = s2 + dl
            s1 = s1.astype(BF16)
            s2 = s2.astype(BF16)
            s_scr[buf_a, 0, r] = s1
            s_scr[buf_a, 1, r] = s2
            m1 = jnp.maximum(m1, jnp.max(slabs(s1, BF16_ROWS), axis=0))
            m2 = jnp.maximum(m2, jnp.max(slabs(s2, BF16_ROWS), axis=0))

        for comp, m in enumerate((m1, m2)):
            m_scr[buf_a, comp] = jnp.broadcast_to(
                jnp.max(m.astype(F32), axis=0, keepdims=True), (BF16_ROWS, t)).astype(BF16)

        l1 = jnp.sum(l1, axis=0, keepdims=True)
        l2 = jnp.sum(l2, axis=0, keepdims=True)
        c_scr[0] = jnp.broadcast_to(1.0 / l1, (F32_ROWS, t))
        c_scr[1] = jnp.broadcast_to(lam * l1 / l2, (F32_ROWS, t))

        ot = (slabs(acc, F32_ROWS) * c1).reshape(V_DIM, t)
        ms = jnp.mean(ot * ot, axis=0, keepdims=True)
        ot = ot * lax.rsqrt(ms + EPS) * sg_ref[...] * (1.0 - lam_init)
        o_ref[qrows, :] = ot.T.astype(BF16)


def _attn(rel_bias, proj, k2a, vt, bias_tiles, lq1, lk1, lq2, lk2, subln_g, *, batch, seq, lam_init):
    t = ATT_TILE
    nq = seq // t
    assert nq >= 3 and nq % ATT_PAIR == 0
    ppq = nq // ATT_PAIR
    n_pairs = batch * N_HEADS * ppq

    def split(pair):
        bh = pair // ppq
        return bh // N_HEADS, lax.rem(bh, N_HEADS), lax.rem(pair, ppq)

    def a_pair(s):
        return split(jnp.minimum(s, n_pairs - 1))

    def c_pair(s):
        return split(jnp.maximum(s - 1, 0))

    def q_map(s):
        b, h, j = a_pair(s)
        return b * ppq + j, h

    def k1_map(s):
        b, h, _ = a_pair(s)
        return b, N_HEADS + h

    def k2_map(s):
        b, h, _ = a_pair(s)
        return b, h

    def v_map(s):
        b, h, _ = c_pair(s)
        return h, b

    def dl_map(s):
        return a_pair(s)[1], 0, 0, 0

    def o_map(s):
        b, h, j = c_pair(s)
        return b * ppq + j, h

    small = lambda shape: pl.BlockSpec(shape, lambda s: (0, 0))
    return pl.pallas_call(
        functools.partial(_attn_kernel, lam_init=lam_init, nq=nq, n_pairs=n_pairs),
        out_shape=jax.ShapeDtypeStruct((batch * seq, N_HEADS * V_DIM), BF16),
        grid=(n_pairs + 1,),
        in_specs=[
            pl.BlockSpec(memory_space=pltpu.SMEM),
            pl.BlockSpec((ATT_PAIR * t, LANES), q_map),
            pl.BlockSpec((seq, LANES), k1_map),
            pl.BlockSpec((seq, LANES), k2_map),
            pl.BlockSpec((V_DIM, seq), v_map),
            pl.BlockSpec((1, 4, t, t), dl_map),
            small((1, HEAD_DIM)), small((1, HEAD_DIM)), small((1, HEAD_DIM)), small((1, HEAD_DIM)),
            small((V_DIM, 1)),
        ],
        out_specs=pl.BlockSpec((ATT_PAIR * t, V_DIM), o_map),
        scratch_shapes=[
            pltpu.VMEM((2, 2, nq, t, t), BF16),
            pltpu.VMEM((2, nq, t, t), BF16),
            pltpu.VMEM((ATT_PAIR, 2, 2, t, LANES), BF16),
            pltpu.VMEM((2, 2, BF16_ROWS, t), BF16),
            pltpu.VMEM((2, F32_ROWS, t), F32),
        ],
        compiler_params=pltpu.CompilerParams(
            dimension_semantics=("arbitrary",), vmem_limit_bytes=VMEM_LIMIT),
        name="attn",
    )(rel_bias, proj, proj, k2a, vt, bias_tiles, lq1, lk1, lq2, lk2, subln_g)


def _post_kernel(o_ref, sga_ref, pc_ref, bg_ref, sma_ref, smb_ref, pcp_ref, pcn_ref, x_ref, cw_ref,
                 wa_ref, wc_ref, wo_ref, out_ref, *, tiles_per_seq):
    tm = out_ref.shape[0]
    pos = lax.rem(pl.program_id(0), tiles_per_seq)

    ua = (o_ref[...].astype(F32) * sga_ref[...].astype(F32)).astype(BF16)
    ya = _dot(ua, wa_ref[...])

    pc = pc_ref[...].astype(F32)
    prev_row = pcp_ref[...].astype(F32)[BF16_ROWS - 1:BF16_ROWS, :]
    next_row = pcn_ref[...].astype(F32)[0:1, :]
    prev_row = jnp.where(pos == 0, 0.0, prev_row)
    next_row = jnp.where(pos == tiles_per_seq - 1, 0.0, next_row)
    row = lax.broadcasted_iota(jnp.int32, pc.shape, 0)
    up = jnp.where(row == 0, prev_row, pltpu.roll(pc, 1, axis=0))
    dn = jnp.where(row == tm - 1, next_row, pltpu.roll(pc, tm - 1, axis=0))
    cw = cw_ref[...]
    cv = up * cw[0:1, :] + pc * cw[1:2, :] + dn * cw[2:3, :]
    ub = (bg_ref[...].astype(F32) * cv).astype(BF16)
    yb = _dot(ub, wc_ref[...])

    merged = (sma_ref[...].astype(F32) * ya + smb_ref[...].astype(F32) * yb).astype(BF16)
    out_ref[...] = x_ref[...] + _dot(merged, wo_ref[...])


def _post(o, proj, x2d, conv_w, wa, wc, wo, *, seq):
    t = x2d.shape[0]
    tm = POST_ROWS
    hb = tm // BF16_ROWS
    n_hb = t // BF16_ROWS
    grp = OUT_GROUPS.index
    col = lambda c: pl.BlockSpec((tm, D_MODEL), lambda m: (m, c))
    prev = lambda c: pl.BlockSpec((BF16_ROWS, D_MODEL), lambda m: (jnp.maximum(m * hb - 1, 0), c))
    nxt = lambda c: pl.BlockSpec((BF16_ROWS, D_MODEL),
                                 lambda m: (jnp.minimum((m + 1) * hb, n_hb - 1), c))
    whole = lambda r: pl.BlockSpec((r, D_MODEL), lambda m: (0, 0))
    return pl.pallas_call(
        functools.partial(_post_kernel, tiles_per_seq=seq // tm),
        out_shape=jax.ShapeDtypeStruct((t, D_MODEL), F32),
        grid=(t // tm,),
        in_specs=[
            col(0),
            col(grp("silu_ga")), col(grp("c_xc")), col(grp("b_silu_gb")),
            col(grp("sig_ma")), col(grp("sig_mb")),
            prev(grp("c_xc")), nxt(grp("c_xc")),
            col(0),
            whole(3), whole(D_MODEL), whole(D_MODEL), whole(D_MODEL),
        ],
        out_specs=pl.BlockSpec((tm, D_MODEL), lambda m: (m, 0)),
        compiler_params=pltpu.CompilerParams(
            dimension_semantics=("arbitrary",), vmem_limit_bytes=VMEM_LIMIT),
        name="post",
    )(o, proj, proj, proj, proj, proj, proj, proj, x2d, conv_w, wa, wc, wo)


def _lambda_init(layer_idx):
    return 0.8 - 0.6 * math.exp(-0.3 * layer_idx)


def _trunk(x, bias_tiles, p):
    batch, seq, _ = x.shape
    x2d = x.reshape(batch * seq, D_MODEL)
    for l in range(DEPTH):
        proj, k2a, vt = _proj(x2d, p["norm_g"][l], p["w_in"][l], p["qg"][l], p["kg"][l])
        o = _attn(p["rel_bias"], proj, k2a, vt, bias_tiles, p["lq1"][l], p["lk1"][l], p["lq2"][l],
                  p["lk2"][l], p["subln_g"][l], batch=batch, seq=seq, lam_init=_lambda_init(l))
        x2d = _post(o, proj, x2d, p["conv_w"][l], p["wa"][l], p["wc"][l], p["wo"][l], seq=seq)
    return x2d.reshape(batch, seq, D_MODEL)


def kernel(x_prompt, x_sample, rel_bias, norm_g, w_in, q_norm_g, k_norm_g, lambda_q1, lambda_k1,
           lambda_q2, lambda_k2, subln_g, w_attn_out, conv_w, w_conv_out, w_o):
    reps = D_MODEL // HEAD_DIM
    p = {
        "rel_bias": rel_bias,
        "norm_g": norm_g.reshape(DEPTH, 1, D_MODEL),
        "w_in": w_in.astype(BF16),
        "qg": jnp.tile(q_norm_g, (1, reps)).reshape(DEPTH, 1, D_MODEL),
        "kg": jnp.tile(k_norm_g, (1, reps)).reshape(DEPTH, 1, D_MODEL),
        "lq1": lambda_q1.reshape(DEPTH, 1, HEAD_DIM),
        "lk1": lambda_k1.reshape(DEPTH, 1, HEAD_DIM),
        "lq2": lambda_q2.reshape(DEPTH, 1, HEAD_DIM),
        "lk2": lambda_k2.reshape(DEPTH, 1, HEAD_DIM),
        "subln_g": subln_g.reshape(DEPTH, V_DIM, 1),
        "conv_w": conv_w,
        "wa": w_attn_out.astype(BF16),
        "wc": w_conv_out.astype(BF16),
        "wo": w_o.astype(BF16),
    }
    bias_tiles = _bias_tiles(rel_bias)
    return (_trunk(x_prompt, bias_tiles, p), _trunk(x_sample, bias_tiles, p))
```

```python
import functools
import math

import jax
import jax.numpy as jnp
from jax import lax
from jax.experimental import pallas as pl
from jax.experimental.pallas import tpu as pltpu

D_MODEL = 1024
DEPTH = 4
N_HEADS = 8
HEAD_DIM = 64
V_DIM = 2 * HEAD_DIM
SCALE = HEAD_DIM ** -0.5
NUM_BUCKETS = 32
MAX_DISTANCE = 128
N_COLS = 10
OUT_GROUPS = ("q", "k1", "silu_ga", "c_xc", "b_silu_gb", "sig_ma", "sig_mb")
EPS = 1e-6
LOG2E = 1.4426950408889634
FAR_NEG_BUCKET = NUM_BUCKETS // 2 - 1
FAR_POS_BUCKET = NUM_BUCKETS - 1

LANES = 128
BF16_ROWS = 16
F32_ROWS = 8
MXU_COLS = 256
PROJ_CHUNK = 2 * MXU_COLS
PROJ_CHUNK_ROWS = 512
ATT_TILE = 512
PROJ_ROWS = 1024
POST_ROWS = 512
ATT_PAIR = 2
VMEM_LIMIT = 56 * 1024 * 1024

F32 = jnp.float32
BF16 = jnp.bfloat16


def _dot(a, b):
    return jnp.dot(a, b, preferred_element_type=F32)


def _dot_nt(a, b):
    return lax.dot_general(a, b, (((1,), (1,)), ((), ())), preferred_element_type=F32)


def _rel_bucket(rel):
    nb = NUM_BUCKETS // 2
    ret = (rel > 0).astype(jnp.int32) * nb
    n = jnp.abs(rel)
    max_exact = nb // 2
    is_small = n < max_exact
    nf = jnp.maximum(n, 1).astype(F32)
    large = max_exact + (jnp.log(nf / max_exact) / math.log(MAX_DISTANCE / max_exact)
                         * (nb - max_exact)).astype(jnp.int32)
    large = jnp.minimum(large, nb - 1)
    return ret + jnp.where(is_small, n, large)


def _bias_kernel(rb_ref, out_ref):
    h = pl.program_id(0)
    t = ATT_TILE
    key = lax.broadcasted_iota(jnp.int32, (t, t), 0)
    qry = lax.broadcasted_iota(jnp.int32, (t, t), 1)
    for v in range(3):
        bucket = _rel_bucket(key - qry + (v - 1) * t)
        val = jnp.zeros((t, t), F32)
        for b in range(NUM_BUCKETS):
            val = jnp.where(bucket == b, rb_ref[b, h], val)
        far = rb_ref[FAR_NEG_BUCKET, h] if v == 0 else rb_ref[FAR_POS_BUCKET, h]
        out_ref[0, v] = (val - far) * LOG2E
    out_ref[0, 3] = jnp.zeros((t, t), F32)


def _bias_tiles(rel_bias):
    t = ATT_TILE
    return pl.pallas_call(
        _bias_kernel,
        out_shape=jax.ShapeDtypeStruct((N_HEADS, 4, t, t), F32),
        grid=(N_HEADS,),
        in_specs=[pl.BlockSpec(memory_space=pltpu.SMEM)],
        out_specs=pl.BlockSpec((1, 4, t, t), lambda h: (h, 0, 0, 0)),
        compiler_params=pltpu.CompilerParams(
            dimension_semantics=("arbitrary",), vmem_limit_bytes=VMEM_LIMIT),
        name="bias_tiles",
    )(rel_bias)


def _proj_kernel(x_ref, ng_ref, w_ref, qg_ref, kg_ref, o_ref, k2_ref, vt_ref, h_scr, keep_scr):
    n = pl.program_id(1)

    @pl.when(n == 0)
    def _():
        xf = x_ref[...]
        ms = jnp.mean(xf * xf, axis=-1, keepdims=True)
        h_scr[...] = (xf * lax.rsqrt(ms + EPS) * ng_ref[...]).astype(BF16)

    rows = h_scr.shape[0]

    def per_chunk(epilogue):
        for r0 in range(0, rows, PROJ_CHUNK_ROWS):
            rs = slice(r0, r0 + PROJ_CHUNK_ROWS)
            h = h_scr[rs, :]
            for c in range(D_MODEL // PROJ_CHUNK):
                acc = _dot(h, w_ref[:, c * PROJ_CHUNK:(c + 1) * PROJ_CHUNK])
                for j in range(PROJ_CHUNK // MXU_COLS):
                    lo = c * PROJ_CHUNK + j * MXU_COLS
                    epilogue(acc[:, j * MXU_COLS:(j + 1) * MXU_COLS], rs, slice(lo, lo + MXU_COLS))

    def qk_norm(acc, gain):
        gi = lax.broadcasted_iota(jnp.int32, (MXU_COLS, MXU_COLS), 0) // HEAD_DIM
        gj = lax.broadcasted_iota(jnp.int32, (MXU_COLS, MXU_COLS), 1) // HEAD_DIM
        ones_bd = (gi == gj).astype(BF16)
        ss = _dot((acc * acc).astype(BF16), ones_bd)
        return (acc * lax.rsqrt(ss * (1.0 / HEAD_DIM) + EPS) * gain).astype(BF16)

    @pl.when(n == 0)
    def _():
        def q_out(acc, rs, cs):
            o_ref[rs, cs] = qk_norm(acc, qg_ref[:, cs] * (SCALE * LOG2E))
        per_chunk(q_out)

    @pl.when(n == 1)
    def _():
        lane = lax.broadcasted_iota(jnp.int32, (PROJ_CHUNK_ROWS, MXU_COLS), 1) & (LANES - 1)
        aug1 = jnp.where((lane >> 1) == HEAD_DIM // 2, 1.0, 0.0).astype(BF16)
        aug2 = jnp.where(lane < 2, 1.0, 0.0).astype(BF16)

        def k_out(acc, rs, cs):
            kb = qk_norm(acc, kg_ref[:, cs])
            o_ref[rs, cs] = jnp.where(lane < HEAD_DIM, kb, aug1)
            k2_ref[rs, cs] = jnp.where(lane < HEAD_DIM, aug2, kb)
        per_chunk(k_out)

    @pl.when(n == 2)
    def _():
        def v_out(acc, rs, cs):
            vt_ref[cs, rs] = acc.T.astype(BF16)
        per_chunk(v_out)

    @pl.when(n == 3)
    def _():
        def silu_out(acc, rs, cs):
            o_ref[rs, cs] = (acc * jax.nn.sigmoid(acc)).astype(BF16)
        per_chunk(silu_out)

    @pl.when((n == 4) | (n == 6))
    def _():
        def keep(acc, rs, cs):
            keep_scr[rs, cs] = acc.astype(BF16)
        per_chunk(keep)

    @pl.when(n == 5)
    def _():
        def conv_in(acc, rs, cs):
            o_ref[rs, cs] = (acc * keep_scr[rs, cs].astype(F32)).astype(BF16)
        per_chunk(conv_in)

    @pl.when(n == 7)
    def _():
        def gate_out(acc, rs, cs):
            o_ref[rs, cs] = (keep_scr[rs, cs].astype(F32) * (acc * jax.nn.sigmoid(acc))).astype(BF16)
        per_chunk(gate_out)

    @pl.when(n >= 8)
    def _():
        def sigmoid_out(acc, rs, cs):
            o_ref[rs, cs] = jax.nn.sigmoid(acc).astype(BF16)
        per_chunk(sigmoid_out)


def _proj(x2d, norm_g, w_in, qg, kg):
    t = x2d.shape[0]
    tm = PROJ_ROWS

    def out_map(m, n):
        skipped = (n >= 3).astype(jnp.int32) + (n >= 5).astype(jnp.int32) + (n >= 7).astype(jnp.int32)
        return m, n - skipped

    return pl.pallas_call(
        _proj_kernel,
        out_shape=(jax.ShapeDtypeStruct((t, len(OUT_GROUPS) * D_MODEL), BF16),
                   jax.ShapeDtypeStruct((t, D_MODEL), BF16),
                   jax.ShapeDtypeStruct((D_MODEL, t), BF16)),
        grid=(t // tm, N_COLS),
        in_specs=[
            pl.BlockSpec((tm, D_MODEL), lambda m, n: (m, 0)),
            pl.BlockSpec((1, D_MODEL), lambda m, n: (0, 0)),
            pl.BlockSpec((D_MODEL, D_MODEL), lambda m, n: (0, n)),
            pl.BlockSpec((1, D_MODEL), lambda m, n: (0, 0)),
            pl.BlockSpec((1, D_MODEL), lambda m, n: (0, 0)),
        ],
        out_specs=(pl.BlockSpec((tm, D_MODEL), out_map),
                   pl.BlockSpec((tm, D_MODEL), lambda m, n: (m, 0)),
                   pl.BlockSpec((D_MODEL, tm), lambda m, n: (0, m))),
        scratch_shapes=[pltpu.VMEM((tm, D_MODEL), BF16), pltpu.VMEM((tm, D_MODEL), BF16)],
        compiler_params=pltpu.CompilerParams(
            dimension_semantics=("arbitrary", "arbitrary"), vmem_limit_bytes=VMEM_LIMIT),
        name="proj",
    )(x2d, norm_g, w_in, qg, kg)


def _attn_kernel(rb_ref, q_ref, k1_ref, k2_ref, vt_ref, dl_ref, lq1_ref, lk1_ref, lq2_ref, lk2_ref,
                 sg_ref, o_ref, s_scr, p_scr, q_scr, m_scr, c_scr, *, lam_init, nq, n_pairs):
    t = ATT_TILE
    step = pl.program_id(0)

    @pl.when(step == 0)
    def _():
        s_scr[...] = jnp.zeros(s_scr.shape, BF16)
        p_scr[...] = jnp.zeros(p_scr.shape, BF16)
        m_scr[...] = jnp.zeros(m_scr.shape, BF16)
        c_scr[...] = jnp.zeros(c_scr.shape, F32)

    pair_a = jnp.minimum(step, n_pairs - 1)
    pair_c = jnp.maximum(step - 2, 0)
    ha = lax.rem(pair_a // (nq // ATT_PAIR), N_HEADS)
    lam = (jnp.exp(jnp.sum(lq1_ref[...] * lk1_ref[...], axis=-1, keepdims=True))
           - jnp.exp(jnp.sum(lq2_ref[...] * lk2_ref[...], axis=-1, keepdims=True))
           + lam_init)

    def slabs(x, rows):
        return x.reshape(x.shape[0] // rows, rows, x.shape[1])

    def key_tile(i, r):
        kt = i + r
        kt = jnp.where(kt >= nq, kt - nq, kt)
        return pl.ds(pl.multiple_of(kt * t, t), t)

    for u in range(ATT_PAIR):
        ia = lax.rem(pair_a * ATT_PAIR + u, nq)
        ic = lax.rem(pair_c * ATT_PAIR + u, nq)
        qrows = slice(u * t, (u + 1) * t)

        q = q_ref[qrows, :]
        lane = lax.broadcasted_iota(jnp.int32, (t, LANES), 1)
        for side, bucket in enumerate((FAR_NEG_BUCKET, FAR_POS_BUCKET)):
            cv = jnp.full((t, LANES), rb_ref[bucket, ha] * LOG2E, F32)
            hi = cv.astype(BF16).astype(F32)
            for comp in range(2):
                base = HEAD_DIM if comp == 0 else 0
                keep = (lane < HEAD_DIM) if comp == 0 else (lane >= HEAD_DIM)
                aug = jnp.where(lane == base, hi, jnp.where(lane == base + 1, cv - hi, 0.0))
                q_scr[u, comp, side] = jnp.where(keep, q, aug.astype(BF16))

        mb1 = m_scr[u, 0]
        mb2 = m_scr[u, 1]
        c1 = c_scr[u, 0]
        rho = jnp.concatenate([c_scr[u, 1], c_scr[u, 1]], axis=0).astype(BF16)

        m1 = m2 = jnp.full((F32_ROWS, t), -jnp.inf, F32)
        l1 = l2 = jnp.zeros((BF16_ROWS, t), F32)
        acc = jnp.zeros((V_DIM, t), F32)
        for r in range(nq):
            a = slabs(p_scr[u, 0, r], BF16_ROWS) - slabs(p_scr[u, 1, r], BF16_ROWS) * rho
            acc = acc + _dot(vt_ref[:, key_tile(ic, r)], a.reshape(t, t))

            def probs(comp, mb):
                p = jnp.exp2((slabs(s_scr[u, comp, r], BF16_ROWS) - mb).reshape(t, t))
                p_scr[u, comp, r] = p
                parts = [p[i * BF16_ROWS:(i + 1) * BF16_ROWS, :] for i in range(t // BF16_ROWS)]
                while len(parts) > 1:
                    parts = [parts[i] + parts[i + 1] for i in range(0, len(parts), 2)]
                return parts[0].astype(F32)

            l1 = l1 + probs(0, mb1)
            l2 = l2 + probs(1, mb2)
            wrapped = ia + r >= nq
            side = 1 if r == 0 else jnp.where(wrapped, 0, 1)
            keys = key_tile(ia, r)
            s1 = _dot_nt(k1_ref[keys, :], q_scr[u, 0, side])
            s2 = _dot_nt(k2_ref[keys, :], q_scr[u, 1, side])
            if r in (0, 1, nq - 1):
                offset = (r + 1) % nq - 1
                dl = dl_ref[0, jnp.where(wrapped == (r == nq - 1), offset + 1, 3)]
                s1 = s1 + dl
                s2 = s2 + dl
            s_scr[u, 0, r] = s1.astype(BF16)
            s_scr[u, 1, r] = s2.astype(BF16)
            m1 = jnp.maximum(m1, jnp.max(slabs(s1, F32_ROWS), axis=0))
            m2 = jnp.maximum(m2, jnp.max(slabs(s2, F32_ROWS), axis=0))

        for comp, m in enumerate((m1, m2)):
            m_scr[u, comp] = jnp.broadcast_to(
                jnp.max(m, axis=0, keepdims=True), (BF16_ROWS, t)).astype(BF16)

        l1 = jnp.sum(l1, axis=0, keepdims=True)
        l2 = jnp.sum(l2, axis=0, keepdims=True)
        c_scr[u, 0] = jnp.broadcast_to(1.0 / l1, (F32_ROWS, t))
        c_scr[u, 1] = jnp.broadcast_to(lam * l1 / l2, (F32_ROWS, t))

        ot = (slabs(acc, F32_ROWS) * c1).reshape(V_DIM, t)
        ms = jnp.mean(ot * ot, axis=0, keepdims=True)
        ot = ot * lax.rsqrt(ms + EPS) * sg_ref[...] * (1.0 - lam_init)
        o_ref[qrows, :] = ot.T.astype(BF16)


def _attn(rel_bias, proj, k2a, vt, bias_tiles, lq1, lk1, lq2, lk2, subln_g, *, batch, seq, lam_init):
    t = ATT_TILE
    nq = seq // t
    assert nq >= 3 and nq % ATT_PAIR == 0
    ppq = nq // ATT_PAIR
    n_pairs = batch * N_HEADS * ppq

    def split(pair):
        bh = pair // ppq
        return bh // N_HEADS, lax.rem(bh, N_HEADS), lax.rem(pair, ppq)

    def a_pair(s):
        return split(jnp.minimum(s, n_pairs - 1))

    def c_pair(s):
        return split(jnp.maximum(s - 2, 0))

    def q_map(s):
        b, h, j = a_pair(s)
        return b * ppq + j, h

    def k1_map(s):
        b, h, _ = a_pair(s)
        return b, N_HEADS + h

    def k2_map(s):
        b, h, _ = a_pair(s)
        return b, h

    def v_map(s):
        b, h, _ = c_pair(s)
        return h, b

    def dl_map(s):
        return a_pair(s)[1], 0, 0, 0

    def o_map(s):
        b, h, j = c_pair(s)
        return b * ppq + j, h

    small = lambda shape: pl.BlockSpec(shape, lambda s: (0, 0))
    return pl.pallas_call(
        functools.partial(_attn_kernel, lam_init=lam_init, nq=nq, n_pairs=n_pairs),
        out_shape=jax.ShapeDtypeStruct((batch * seq, N_HEADS * V_DIM), BF16),
        grid=(n_pairs + 2,),
        in_specs=[
            pl.BlockSpec(memory_space=pltpu.SMEM),
            pl.BlockSpec((ATT_PAIR * t, LANES), q_map),
            pl.BlockSpec((seq, LANES), k1_map),
            pl.BlockSpec((seq, LANES), k2_map),
            pl.BlockSpec((V_DIM, seq), v_map),
            pl.BlockSpec((1, 4, t, t), dl_map),
            small((1, HEAD_DIM)), small((1, HEAD_DIM)), small((1, HEAD_DIM)), small((1, HEAD_DIM)),
            small((V_DIM, 1)),
        ],
        out_specs=pl.BlockSpec((ATT_PAIR * t, V_DIM), o_map),
        scratch_shapes=[
            pltpu.VMEM((ATT_PAIR, 2, nq, t, t), BF16),
            pltpu.VMEM((ATT_PAIR, 2, nq, t, t), BF16),
            pltpu.VMEM((ATT_PAIR, 2, 2, t, LANES), BF16),
            pltpu.VMEM((ATT_PAIR, 2, BF16_ROWS, t), BF16),
            pltpu.VMEM((ATT_PAIR, 2, F32_ROWS, t), F32),
        ],
        compiler_params=pltpu.CompilerParams(
            dimension_semantics=("arbitrary",), vmem_limit_bytes=VMEM_LIMIT),
        name="attn",
    )(rel_bias, proj, proj, k2a, vt, bias_tiles, lq1, lk1, lq2, lk2, subln_g)


def _post_kernel(o_ref, sga_ref, pc_ref, bg_ref, sma_ref, smb_ref, pcp_ref, pcn_ref, x_ref, cw_ref,
                 wa_ref, wc_ref, wo_ref, out_ref, *, tiles_per_seq):
    tm = out_ref.shape[0]
    pos = lax.rem(pl.program_id(0), tiles_per_seq)

    ua = (o_ref[...].astype(F32) * sga_ref[...].astype(F32)).astype(BF16)
    ya = _dot(ua, wa_ref[...])

    pc = pc_ref[...].astype(F32)
    prev_row = pcp_ref[...].astype(F32)[BF16_ROWS - 1:BF16_ROWS, :]
    next_row = pcn_ref[...].astype(F32)[0:1, :]
    prev_row = jnp.where(pos == 0, 0.0, prev_row)
    next_row = jnp.where(pos == tiles_per_seq - 1, 0.0, next_row)
    row = lax.broadcasted_iota(jnp.int32, pc.shape, 0)
    up = jnp.where(row == 0, prev_row, pltpu.roll(pc, 1, axis=0))
    dn = jnp.where(row == tm - 1, next_row, pltpu.roll(pc, tm - 1, axis=0))
    cw = cw_ref[...]
    cv = up * cw[0:1, :] + pc * cw[1:2, :] + dn * cw[2:3, :]
    ub = (bg_ref[...].astype(F32) * cv).astype(BF16)
    yb = _dot(ub, wc_ref[...])

    merged = (sma_ref[...].astype(F32) * ya + smb_ref[...].astype(F32) * yb).astype(BF16)
    out_ref[...] = x_ref[...] + _dot(merged, wo_ref[...])


def _post(o, proj, x2d, conv_w, wa, wc, wo, *, seq):
    t = x2d.shape[0]
    tm = POST_ROWS
    hb = tm // BF16_ROWS
    n_hb = t // BF16_ROWS
    grp = OUT_GROUPS.index
    col = lambda c: pl.BlockSpec((tm, D_MODEL), lambda m: (m, c))
    prev = lambda c: pl.BlockSpec((BF16_ROWS, D_MODEL), lambda m: (jnp.maximum(m * hb - 1, 0), c))
    nxt = lambda c: pl.BlockSpec((BF16_ROWS, D_MODEL),
                                 lambda m: (jnp.minimum((m + 1) * hb, n_hb - 1), c))
    whole = lambda r: pl.BlockSpec((r, D_MODEL), lambda m: (0, 0))
    return pl.pallas_call(
        functools.partial(_post_kernel, tiles_per_seq=seq // tm),
        out_shape=jax.ShapeDtypeStruct((t, D_MODEL), F32),
        grid=(t // tm,),
        in_specs=[
            col(0),
            col(grp("silu_ga")), col(grp("c_xc")), col(grp("b_silu_gb")),
            col(grp("sig_ma")), col(grp("sig_mb")),
            prev(grp("c_xc")), nxt(grp("c_xc")),
            col(0),
            whole(3), whole(D_MODEL), whole(D_MODEL), whole(D_MODEL),
        ],
        out_specs=pl.BlockSpec((tm, D_MODEL), lambda m: (m, 0)),
        compiler_params=pltpu.CompilerParams(
            dimension_semantics=("arbitrary",), vmem_limit_bytes=VMEM_LIMIT),
        name="post",
    )(o, proj, proj, proj, proj, proj, proj, proj, x2d, conv_w, wa, wc, wo)


def _lambda_init(layer_idx):
    return 0.8 - 0.6 * math.exp(-0.3 * layer_idx)


def _trunk(x, bias_tiles, p):
    batch, seq, _ = x.shape
    x2d = x.reshape(batch * seq, D_MODEL)
    for l in range(DEPTH):
        proj, k2a, vt = _proj(x2d, p["norm_g"][l], p["w_in"][l], p["qg"][l], p["kg"][l])
        o = _attn(p["rel_bias"], proj, k2a, vt, bias_tiles, p["lq1"][l], p["lk1"][l], p["lq2"][l],
                  p["lk2"][l], p["subln_g"][l], batch=batch, seq=seq, lam_init=_lambda_init(l))
        x2d = _post(o, proj, x2d, p["conv_w"][l], p["wa"][l], p["wc"][l], p["wo"][l], seq=seq)
    return x2d.reshape(batch, seq, D_MODEL)


def kernel(x_prompt, x_sample, rel_bias, norm_g, w_in, q_norm_g, k_norm_g, lambda_q1, lambda_k1,
           lambda_q2, lambda_k2, subln_g, w_attn_out, conv_w, w_conv_out, w_o):
    reps = D_MODEL // HEAD_DIM
    p = {
        "rel_bias": rel_bias,
        "norm_g": norm_g.reshape(DEPTH, 1, D_MODEL),
        "w_in": w_in.astype(BF16),
        "qg": jnp.tile(q_norm_g, (1, reps)).reshape(DEPTH, 1, D_MODEL),
        "kg": jnp.tile(k_norm_g, (1, reps)).reshape(DEPTH, 1, D_MODEL),
        "lq1": lambda_q1.reshape(DEPTH, 1, HEAD_DIM),
        "lk1": lambda_k1.reshape(DEPTH, 1, HEAD_DIM),
        "lq2": lambda_q2.reshape(DEPTH, 1, HEAD_DIM),
        "lk2": lambda_k2.reshape(DEPTH, 1, HEAD_DIM),
        "subln_g": subln_g.reshape(DEPTH, V_DIM, 1),
        "conv_w": conv_w,
        "wa": w_attn_out.astype(BF16),
        "wc": w_conv_out.astype(BF16),
        "wo": w_o.astype(BF16),
    }
    bias_tiles = _bias_tiles(rel_bias)
    return (_trunk(x_prompt, bias_tiles, p), _trunk(x_sample, bias_tiles, p))
```

```python
import functools
import math

import jax
import jax.numpy as jnp
from jax import lax
from jax.experimental import pallas as pl
from jax.experimental.pallas import tpu as pltpu

D_MODEL = 1024
DEPTH = 4
N_HEADS = 8
HEAD_DIM = 64
V_DIM = 2 * HEAD_DIM
SCALE = HEAD_DIM ** -0.5
NUM_BUCKETS = 32
MAX_DISTANCE = 128
N_COLS = 10
OUT_GROUPS = ("q", "k1", "silu_ga", "c_xc", "b_silu_gb", "sig_ma", "sig_mb")
EPS = 1e-6
LOG2E = 1.4426950408889634
FAR_NEG_BUCKET = NUM_BUCKETS // 2 - 1
FAR_POS_BUCKET = NUM_BUCKETS - 1

LANES = 128
BF16_ROWS = 16
F32_ROWS = 8
MXU_COLS = 256
PROJ_CHUNK = 2 * MXU_COLS
PROJ_CHUNK_ROWS = 512
ATT_TILE = 512
PROJ_ROWS = 1024
POST_ROWS = 512
ATT_PAIR = 2
VMEM_LIMIT = 56 * 1024 * 1024

F32 = jnp.float32
BF16 = jnp.bfloat16


def _dot(a, b):
    return jnp.dot(a, b, preferred_element_type=F32)


def _sigmoid(x):
    return 0.5 * jnp.tanh(0.5 * x) + 0.5


def _dot_nt(a, b):
    return lax.dot_general(a, b, (((1,), (1,)), ((), ())), preferred_element_type=F32)


def _rel_bucket(rel):
    nb = NUM_BUCKETS // 2
    ret = (rel > 0).astype(jnp.int32) * nb
    n = jnp.abs(rel)
    max_exact = nb // 2
    is_small = n < max_exact
    nf = jnp.maximum(n, 1).astype(F32)
    large = max_exact + (jnp.log(nf / max_exact) / math.log(MAX_DISTANCE / max_exact)
                         * (nb - max_exact)).astype(jnp.int32)
    large = jnp.minimum(large, nb - 1)
    return ret + jnp.where(is_small, n, large)


def _bias_kernel(rb_ref, out_ref):
    h = pl.program_id(0)
    t = ATT_TILE
    key = lax.broadcasted_iota(jnp.int32, (t, t), 0)
    qry = lax.broadcasted_iota(jnp.int32, (t, t), 1)
    for v in range(3):
        bucket = _rel_bucket(key - qry + (v - 1) * t)
        val = jnp.zeros((t, t), F32)
        for b in range(NUM_BUCKETS):
            val = jnp.where(bucket == b, rb_ref[b, h], val)
        far = rb_ref[FAR_NEG_BUCKET, h] if v == 0 else rb_ref[FAR_POS_BUCKET, h]
        out_ref[0, v] = (val - far) * LOG2E
    out_ref[0, 3] = jnp.zeros((t, t), F32)


def _bias_tiles(rel_bias):
    t = ATT_TILE
    return pl.pallas_call(
        _bias_kernel,
        out_shape=jax.ShapeDtypeStruct((N_HEADS, 4, t, t), F32),
        grid=(N_HEADS,),
        in_specs=[pl.BlockSpec(memory_space=pltpu.SMEM)],
        out_specs=pl.BlockSpec((1, 4, t, t), lambda h: (h, 0, 0, 0)),
        compiler_params=pltpu.CompilerParams(
            dimension_semantics=("arbitrary",), vmem_limit_bytes=VMEM_LIMIT),
        name="bias_tiles",
    )(rel_bias)


def _proj_kernel(x_ref, ng_ref, w_ref, qg_ref, kg_ref, o_ref, k2_ref, vt_ref, h_scr, keep_scr):
    n = pl.program_id(1)

    @pl.when(n == 0)
    def _():
        xf = x_ref[...]
        ms = jnp.mean(xf * xf, axis=-1, keepdims=True)
        h_scr[...] = (xf * lax.rsqrt(ms + EPS) * ng_ref[...]).astype(BF16)

    rows = h_scr.shape[0]

    def per_chunk(epilogue):
        for r0 in range(0, rows, PROJ_CHUNK_ROWS):
            rs = slice(r0, r0 + PROJ_CHUNK_ROWS)
            h = h_scr[rs, :]
            for c in range(D_MODEL // PROJ_CHUNK):
                acc = _dot(h, w_ref[:, c * PROJ_CHUNK:(c + 1) * PROJ_CHUNK])
                for j in range(PROJ_CHUNK // MXU_COLS):
                    lo = c * PROJ_CHUNK + j * MXU_COLS
                    epilogue(acc[:, j * MXU_COLS:(j + 1) * MXU_COLS], rs, slice(lo, lo + MXU_COLS))

    def qk_norm(acc, gain):
        gi = lax.broadcasted_iota(jnp.int32, (MXU_COLS, MXU_COLS), 0) // HEAD_DIM
        gj = lax.broadcasted_iota(jnp.int32, (MXU_COLS, MXU_COLS), 1) // HEAD_DIM
        ones_bd = (gi == gj).astype(BF16)
        ss = _dot((acc * acc).astype(BF16), ones_bd)
        return (acc * lax.rsqrt(ss * (1.0 / HEAD_DIM) + EPS) * gain).astype(BF16)

    @pl.when(n == 0)
    def _():
        def q_out(acc, rs, cs):
            o_ref[rs, cs] = qk_norm(acc, qg_ref[:, cs] * (SCALE * LOG2E))
        per_chunk(q_out)

    @pl.when(n == 1)
    def _():
        lane = lax.broadcasted_iota(jnp.int32, (PROJ_CHUNK_ROWS, MXU_COLS), 1) & (LANES - 1)
        aug1 = jnp.where((lane >> 1) == HEAD_DIM // 2, 1.0, 0.0).astype(BF16)
        aug2 = jnp.where(lane < 2, 1.0, 0.0).astype(BF16)

        def k_out(acc, rs, cs):
            kb = qk_norm(acc, kg_ref[:, cs])
            o_ref[rs, cs] = jnp.where(lane < HEAD_DIM, kb, aug1)
            k2_ref[rs, cs] = jnp.where(lane < HEAD_DIM, aug2, kb)
        per_chunk(k_out)

    @pl.when(n == 2)
    def _():
        def v_out(acc, rs, cs):
            vt_ref[cs, rs] = acc.T.astype(BF16)
        per_chunk(v_out)

    @pl.when(n == 3)
    def _():
        def silu_out(acc, rs, cs):
            o_ref[rs, cs] = (acc * _sigmoid(acc)).astype(BF16)
        per_chunk(silu_out)

    @pl.when((n == 4) | (n == 6))
    def _():
        def keep(acc, rs, cs):
            keep_scr[rs, cs] = acc.astype(BF16)
        per_chunk(keep)

    @pl.when(n == 5)
    def _():
        def conv_in(acc, rs, cs):
            o_ref[rs, cs] = (acc * keep_scr[rs, cs].astype(F32)).astype(BF16)
        per_chunk(conv_in)

    @pl.when(n == 7)
    def _():
        def gate_out(acc, rs, cs):
            o_ref[rs, cs] = (keep_scr[rs, cs].astype(F32) * (acc * _sigmoid(acc))).astype(BF16)
        per_chunk(gate_out)

    @pl.when(n >= 8)
    def _():
        def sigmoid_out(acc, rs, cs):
            o_ref[rs, cs] = _sigmoid(acc).astype(BF16)
        per_chunk(sigmoid_out)


def _proj(x2d, norm_g, w_in, qg, kg):
    t = x2d.shape[0]
    tm = PROJ_ROWS

    def out_map(m, n):
        skipped = (n >= 3).astype(jnp.int32) + (n >= 5).astype(jnp.int32) + (n >= 7).astype(jnp.int32)
        return m, n - skipped

    return pl.pallas_call(
        _proj_kernel,
        out_shape=(jax.ShapeDtypeStruct((t, len(OUT_GROUPS) * D_MODEL), BF16),
                   jax.ShapeDtypeStruct((t, D_MODEL), BF16),
                   jax.ShapeDtypeStruct((D_MODEL, t), BF16)),
        grid=(t // tm, N_COLS),
        in_specs=[
            pl.BlockSpec((tm, D_MODEL), lambda m, n: (m, 0)),
            pl.BlockSpec((1, D_MODEL), lambda m, n: (0, 0)),
            pl.BlockSpec((D_MODEL, D_MODEL), lambda m, n: (0, n)),
            pl.BlockSpec((1, D_MODEL), lambda m, n: (0, 0)),
            pl.BlockSpec((1, D_MODEL), lambda m, n: (0, 0)),
        ],
        out_specs=(pl.BlockSpec((tm, D_MODEL), out_map),
                   pl.BlockSpec((tm, D_MODEL), lambda m, n: (m, 0)),
                   pl.BlockSpec((D_MODEL, tm), lambda m, n: (0, m))),
        scratch_shapes=[pltpu.VMEM((tm, D_MODEL), BF16), pltpu.VMEM((tm, D_MODEL), BF16)],
        compiler_params=pltpu.CompilerParams(
            dimension_semantics=("arbitrary", "arbitrary"), vmem_limit_bytes=VMEM_LIMIT),
        name="proj",
    )(x2d, norm_g, w_in, qg, kg)


def _attn_kernel(rb_ref, q_ref, k1_ref, k2_ref, vt_ref, dl_ref, lq1_ref, lk1_ref, lq2_ref, lk2_ref,
                 sg_ref, o_ref, s_scr, p_scr, q_scr, m_scr, c_scr, *, lam_init, nq, n_pairs):
    t = ATT_TILE
    step = pl.program_id(0)

    @pl.when(step == 0)
    def _():
        s_scr[...] = jnp.zeros(s_scr.shape, BF16)
        p_scr[...] = jnp.zeros(p_scr.shape, BF16)
        m_scr[...] = jnp.zeros(m_scr.shape, BF16)
        c_scr[...] = jnp.zeros(c_scr.shape, F32)

    pair_a = jnp.minimum(step, n_pairs - 1)
    pair_c = jnp.maximum(step - 2, 0)
    ha = lax.rem(pair_a // (nq // ATT_PAIR), N_HEADS)
    lam = (jnp.exp(jnp.sum(lq1_ref[...] * lk1_ref[...], axis=-1, keepdims=True))
           - jnp.exp(jnp.sum(lq2_ref[...] * lk2_ref[...], axis=-1, keepdims=True))
           + lam_init)

    def slabs(x, rows):
        return x.reshape(x.shape[0] // rows, rows, x.shape[1])

    def key_tile(i, r):
        kt = i + r
        kt = jnp.where(kt >= nq, kt - nq, kt)
        return pl.ds(pl.multiple_of(kt * t, t), t)

    def prepare(u):
        ia = lax.rem(pair_a * ATT_PAIR + u, nq)
        ic = lax.rem(pair_c * ATT_PAIR + u, nq)

        q = q_ref[u * t:(u + 1) * t, :]
        lane = lax.broadcasted_iota(jnp.int32, (t, LANES), 1)
        for side, bucket in enumerate((FAR_NEG_BUCKET, FAR_POS_BUCKET)):
            cv = jnp.full((t, LANES), rb_ref[bucket, ha] * LOG2E, F32)
            hi = cv.astype(BF16).astype(F32)
            for comp in range(2):
                base = HEAD_DIM if comp == 0 else 0
                keep = (lane < HEAD_DIM) if comp == 0 else (lane >= HEAD_DIM)
                aug = jnp.where(lane == base, hi, jnp.where(lane == base + 1, cv - hi, 0.0))
                q_scr[u, comp, side] = jnp.where(keep, q, aug.astype(BF16))

        mb1 = m_scr[u, 0]
        mb2 = m_scr[u, 1]
        c1 = c_scr[u, 0]
        rho = jnp.concatenate([c_scr[u, 1], c_scr[u, 1]], axis=0).astype(BF16)

        m0 = jnp.full((F32_ROWS, t), -jnp.inf, F32)
        l0 = jnp.zeros((BF16_ROWS, t), F32)
        return (ia, ic, mb1, mb2, c1, rho), (m0, m0, l0, l0, jnp.zeros((V_DIM, t), F32))

    def tile(u, r, consts, carry):
        ia, ic, mb1, mb2, _, rho = consts
        m1, m2, l1, l2, acc = carry
        a = slabs(p_scr[u, 0, r], BF16_ROWS) - slabs(p_scr[u, 1, r], BF16_ROWS) * rho
        acc = acc + _dot(vt_ref[:, key_tile(ic, r)], a.reshape(t, t))

        def probs(comp, mb):
            p = jnp.exp2((slabs(s_scr[u, comp, r], BF16_ROWS) - mb).reshape(t, t))
            p_scr[u, comp, r] = p
            parts = [p[i * BF16_ROWS:(i + 1) * BF16_ROWS, :] for i in range(t // BF16_ROWS)]
            while len(parts) > 1:
                parts = [parts[i] + parts[i + 1] for i in range(0, len(parts), 2)]
            return parts[0].astype(F32)

        l1 = l1 + probs(0, mb1)
        l2 = l2 + probs(1, mb2)
        wrapped = ia + r >= nq
        side = 1 if r == 0 else jnp.where(wrapped, 0, 1)
        keys = key_tile(ia, r)
        s1 = _dot_nt(k1_ref[keys, :], q_scr[u, 0, side])
        s2 = _dot_nt(k2_ref[keys, :], q_scr[u, 1, side])
        if r in (0, 1, nq - 1):
            offset = (r + 1) % nq - 1
            dl = dl_ref[0, jnp.where(wrapped == (r == nq - 1), offset + 1, 3)]
            s1 = s1 + dl
            s2 = s2 + dl
        s_scr[u, 0, r] = s1.astype(BF16)
        s_scr[u, 1, r] = s2.astype(BF16)
        m1 = jnp.maximum(m1, jnp.max(slabs(s1, F32_ROWS), axis=0))
        m2 = jnp.maximum(m2, jnp.max(slabs(s2, F32_ROWS), axis=0))
        return m1, m2, l1, l2, acc

    def finish(u, consts, carry):
        c1 = consts[4]
        m1, m2, l1, l2, acc = carry
        for comp, m in enumerate((m1, m2)):
            m_scr[u, comp] = jnp.broadcast_to(
                jnp.max(m, axis=0, keepdims=True), (BF16_ROWS, t)).astype(BF16)

        l1 = jnp.sum(l1, axis=0, keepdims=True)
        l2 = jnp.sum(l2, axis=0, keepdims=True)
        c_scr[u, 0] = jnp.broadcast_to(1.0 / l1, (F32_ROWS, t))
        c_scr[u, 1] = jnp.broadcast_to(lam * l1 / l2, (F32_ROWS, t))

        ot = (slabs(acc, F32_ROWS) * c1).reshape(V_DIM, t)
        ms = jnp.mean(ot * ot, axis=0, keepdims=True)
        ot = ot * lax.rsqrt(ms + EPS) * sg_ref[...] * (1.0 - lam_init)
        o_ref[u * t:(u + 1) * t, :] = ot.T.astype(BF16)

    blocks = [prepare(u) for u in range(ATT_PAIR)]
    for u, (consts, carry) in enumerate(blocks):
        for r in range(nq):
            carry = tile(u, r, consts, carry)
        finish(u, consts, carry)


def _attn(rel_bias, proj, k2a, vt, bias_tiles, lq1, lk1, lq2, lk2, subln_g, *, batch, seq, lam_init):
    t = ATT_TILE
    nq = seq // t
    assert nq >= 3 and nq % ATT_PAIR == 0
    ppq = nq // ATT_PAIR
    n_pairs = batch * N_HEADS * ppq

    def split(pair):
        bh = pair // ppq
        return bh // N_HEADS, lax.rem(bh, N_HEADS), lax.rem(pair, ppq)

    def a_pair(s):
        return split(jnp.minimum(s, n_pairs - 1))

    def c_pair(s):
        return split(jnp.maximum(s - 2, 0))

    def q_map(s):
        b, h, j = a_pair(s)
        return b * ppq + j, h

    def k1_map(s):
        b, h, _ = a_pair(s)
        return b, N_HEADS + h

    def k2_map(s):
        b, h, _ = a_pair(s)
        return b, h

    def v_map(s):
        b, h, _ = c_pair(s)
        return h, b

    def dl_map(s):
        return a_pair(s)[1], 0, 0, 0

    def o_map(s):
        b, h, j = c_pair(s)
        return b * ppq + j, h

    small = lambda shape: pl.BlockSpec(shape, lambda s: (0, 0))
    return pl.pallas_call(
        functools.partial(_attn_kernel, lam_init=lam_init, nq=nq, n_pairs=n_pairs),
        out_shape=jax.ShapeDtypeStruct((batch * seq, N_HEADS * V_DIM), BF16),
        grid=(n_pairs + 2,),
        in_specs=[
            pl.BlockSpec(memory_space=pltpu.SMEM),
            pl.BlockSpec((ATT_PAIR * t, LANES), q_map),
            pl.BlockSpec((seq, LANES), k1_map),
            pl.BlockSpec((seq, LANES), k2_map),
            pl.BlockSpec((V_DIM, seq), v_map),
            pl.BlockSpec((1, 4, t, t), dl_map),
            small((1, HEAD_DIM)), small((1, HEAD_DIM)), small((1, HEAD_DIM)), small((1, HEAD_DIM)),
            small((V_DIM, 1)),
        ],
        out_specs=pl.BlockSpec((ATT_PAIR * t, V_DIM), o_map),
        scratch_shapes=[
            pltpu.VMEM((ATT_PAIR, 2, nq, t, t), BF16),
            pltpu.VMEM((ATT_PAIR, 2, nq, t, t), BF16),
            pltpu.VMEM((ATT_PAIR, 2, 2, t, LANES), BF16),
            pltpu.VMEM((ATT_PAIR, 2, BF16_ROWS, t), BF16),
            pltpu.VMEM((ATT_PAIR, 2, F32_ROWS, t), F32),
        ],
        compiler_params=pltpu.CompilerParams(
            dimension_semantics=("arbitrary",), vmem_limit_bytes=VMEM_LIMIT),
        name="attn",
    )(rel_bias, proj, proj, k2a, vt, bias_tiles, lq1, lk1, lq2, lk2, subln_g)


def _post_kernel(o_ref, sga_ref, pc_ref, bg_ref, sma_ref, smb_ref, pcp_ref, pcn_ref, x_ref, cw_ref,
                 wa_ref, wc_ref, wo_ref, out_ref, *, tiles_per_seq):
    tm = out_ref.shape[0]
    pos = lax.rem(pl.program_id(0), tiles_per_seq)

    ua = (o_ref[...].astype(F32) * sga_ref[...].astype(F32)).astype(BF16)
    ya = _dot(ua, wa_ref[...])

    pc = pc_ref[...].astype(F32)
    prev_row = pcp_ref[...].astype(F32)[BF16_ROWS - 1:BF16_ROWS, :]
    next_row = pcn_ref[...].astype(F32)[0:1, :]
    prev_row = jnp.where(pos == 0, 0.0, prev_row)
    next_row = jnp.where(pos == tiles_per_seq - 1, 0.0, next_row)
    row = lax.broadcasted_iota(jnp.int32, pc.shape, 0)
    up = jnp.where(row == 0, prev_row, pltpu.roll(pc, 1, axis=0))
    dn = jnp.where(row == tm - 1, next_row, pltpu.roll(pc, tm - 1, axis=0))
    cw = cw_ref[...]
    cv = up * cw[0:1, :] + pc * cw[1:2, :] + dn * cw[2:3, :]
    ub = (bg_ref[...].astype(F32) * cv).astype(BF16)
    yb = _dot(ub, wc_ref[...])

    merged = (sma_ref[...].astype(F32) * ya + smb_ref[...].astype(F32) * yb).astype(BF16)
    out_ref[...] = x_ref[...] + _dot(merged, wo_ref[...])


def _post(o, proj, x2d, conv_w, wa, wc, wo, *, seq):
    t = x2d.shape[0]
    tm = POST_ROWS
    hb = tm // BF16_ROWS
    n_hb = t // BF16_ROWS
    grp = OUT_GROUPS.index
    col = lambda c: pl.BlockSpec((tm, D_MODEL), lambda m: (m, c))
    prev = lambda c: pl.BlockSpec((BF16_ROWS, D_MODEL), lambda m: (jnp.maximum(m * hb - 1, 0), c))
    nxt = lambda c: pl.BlockSpec((BF16_ROWS, D_MODEL),
                                 lambda m: (jnp.minimum((m + 1) * hb, n_hb - 1), c))
    whole = lambda r: pl.BlockSpec((r, D_MODEL), lambda m: (0, 0))
    return pl.pallas_call(
        functools.partial(_post_kernel, tiles_per_seq=seq // tm),
        out_shape=jax.ShapeDtypeStruct((t, D_MODEL), F32),
        grid=(t // tm,),
        in_specs=[
            col(0),
            col(grp("silu_ga")), col(grp("c_xc")), col(grp("b_silu_gb")),
            col(grp("sig_ma")), col(grp("sig_mb")),
            prev(grp("c_xc")), nxt(grp("c_xc")),
            col(0),
            whole(3), whole(D_MODEL), whole(D_MODEL), whole(D_MODEL),
        ],
        out_specs=pl.BlockSpec((tm, D_MODEL), lambda m: (m, 0)),
        compiler_params=pltpu.CompilerParams(
            dimension_semantics=("arbitrary",), vmem_limit_bytes=VMEM_LIMIT),
        name="post",
    )(o, proj, proj, proj, proj, proj, proj, proj, x2d, conv_w, wa, wc, wo)


def _lambda_init(layer_idx):
    return 0.8 - 0.6 * math.exp(-0.3 * layer_idx)


def _trunk(x, bias_tiles, p):
    batch, seq, _ = x.shape
    x2d = x.reshape(batch * seq, D_MODEL)
    for l in range(DEPTH):
        proj, k2a, vt = _proj(x2d, p["norm_g"][l], p["w_in"][l], p["qg"][l], p["kg"][l])
        o = _attn(p["rel_bias"], proj, k2a, vt, bias_tiles, p["lq1"][l], p["lk1"][l], p["lq2"][l],
                  p["lk2"][l], p["subln_g"][l], batch=batch, seq=seq, lam_init=_lambda_init(l))
        x2d = _post(o, proj, x2d, p["conv_w"][l], p["wa"][l], p["wc"][l], p["wo"][l], seq=seq)
    return x2d.reshape(batch, seq, D_MODEL)


def kernel(x_prompt, x_sample, rel_bias, norm_g, w_in, q_norm_g, k_norm_g, lambda_q1, lambda_k1,
           lambda_q2, lambda_k2, subln_g, w_attn_out, conv_w, w_conv_out, w_o):
    reps = D_MODEL // HEAD_DIM
    p = {
        "rel_bias": rel_bias,
        "norm_g": norm_g.reshape(DEPTH, 1, D_MODEL),
        "w_in": w_in.astype(BF16),
        "qg": jnp.tile(q_norm_g, (1, reps)).reshape(DEPTH, 1, D_MODEL),
        "kg": jnp.tile(k_norm_g, (1, reps)).reshape(DEPTH, 1, D_MODEL),
        "lq1": lambda_q1.reshape(DEPTH, 1, HEAD_DIM),
        "lk1": lambda_k1.reshape(DEPTH, 1, HEAD_DIM),
        "lq2": lambda_q2.reshape(DEPTH, 1, HEAD_DIM),
        "lk2": lambda_k2.reshape(DEPTH, 1, HEAD_DIM),
        "subln_g": subln_g.reshape(DEPTH, V_DIM, 1),
        "conv_w": conv_w,
        "wa": w_attn_out.astype(BF16),
        "wc": w_conv_out.astype(BF16),
        "wo": w_o.astype(BF16),
    }
    bias_tiles = _bias_tiles(rel_bias)
    return (_trunk(x_prompt, bias_tiles, p), _trunk(x_sample, bias_tiles, p))
```

```python
import functools
import math

import jax
import jax.numpy as jnp
from jax import lax
from jax.experimental import pallas as pl
from jax.experimental.pallas import tpu as pltpu

D_MODEL = 1024
DEPTH = 4
N_HEADS = 8
HEAD_DIM = 64
V_DIM = 2 * HEAD_DIM
SCALE = HEAD_DIM ** -0.5
NUM_BUCKETS = 32
MAX_DISTANCE = 128
N_COLS = 10
OUT_GROUPS = ("q", "k1", "silu_ga", "c_xc", "b_silu_gb", "sig_ma", "sig_mb")
EPS = 1e-6
LOG2E = 1.4426950408889634
FAR_NEG_BUCKET = NUM_BUCKETS // 2 - 1
FAR_POS_BUCKET = NUM_BUCKETS - 1

LANES = 128
BF16_ROWS = 16
F32_ROWS = 8
MXU_COLS = 256
PROJ_CHUNK = 2 * MXU_COLS
PROJ_CHUNK_ROWS = 512
ATT_TILE = 512
PROJ_ROWS = 1024
POST_ROWS = 512
ATT_PAIR = 2
VMEM_LIMIT = 56 * 1024 * 1024

F32 = jnp.float32
BF16 = jnp.bfloat16


def _dot(a, b):
    return jnp.dot(a, b, preferred_element_type=F32)


def _sigmoid(x):
    return 0.5 * jnp.tanh(0.5 * x) + 0.5


def _dot_nt(a, b):
    return lax.dot_general(a, b, (((1,), (1,)), ((), ())), preferred_element_type=F32)


def _rel_bucket(rel):
    nb = NUM_BUCKETS // 2
    ret = (rel > 0).astype(jnp.int32) * nb
    n = jnp.abs(rel)
    max_exact = nb // 2
    is_small = n < max_exact
    nf = jnp.maximum(n, 1).astype(F32)
    large = max_exact + (jnp.log(nf / max_exact) / math.log(MAX_DISTANCE / max_exact)
                         * (nb - max_exact)).astype(jnp.int32)
    large = jnp.minimum(large, nb - 1)
    return ret + jnp.where(is_small, n, large)


def _bias_kernel(rb_ref, out_ref):
    h = pl.program_id(0)
    t = ATT_TILE
    key = lax.broadcasted_iota(jnp.int32, (t, t), 0)
    qry = lax.broadcasted_iota(jnp.int32, (t, t), 1)
    for v in range(3):
        bucket = _rel_bucket(key - qry + (v - 1) * t)
        val = jnp.zeros((t, t), F32)
        for b in range(NUM_BUCKETS):
            val = jnp.where(bucket == b, rb_ref[b, h], val)
        far = rb_ref[FAR_NEG_BUCKET, h] if v == 0 else rb_ref[FAR_POS_BUCKET, h]
        out_ref[0, v] = (val - far) * LOG2E
    out_ref[0, 3] = jnp.zeros((t, t), F32)


def _bias_tiles(rel_bias):
    t = ATT_TILE
    return pl.pallas_call(
        _bias_kernel,
        out_shape=jax.ShapeDtypeStruct((N_HEADS, 4, t, t), F32),
        grid=(N_HEADS,),
        in_specs=[pl.BlockSpec(memory_space=pltpu.SMEM)],
        out_specs=pl.BlockSpec((1, 4, t, t), lambda h: (h, 0, 0, 0)),
        compiler_params=pltpu.CompilerParams(
            dimension_semantics=("arbitrary",), vmem_limit_bytes=VMEM_LIMIT),
        name="bias_tiles",
    )(rel_bias)


def _proj_kernel(x_ref, ng_ref, w_ref, qg_ref, kg_ref, o_ref, k2_ref, vt_ref, h_scr, keep_scr):
    n = pl.program_id(1)

    @pl.when(n == 0)
    def _():
        xf = x_ref[...]
        ms = jnp.mean(xf * xf, axis=-1, keepdims=True)
        h_scr[...] = (xf * lax.rsqrt(ms + EPS) * ng_ref[...]).astype(BF16)

    rows = h_scr.shape[0]

    def per_chunk(epilogue):
        for r0 in range(0, rows, PROJ_CHUNK_ROWS):
            rs = slice(r0, r0 + PROJ_CHUNK_ROWS)
            h = h_scr[rs, :]
            for c in range(D_MODEL // PROJ_CHUNK):
                acc = _dot(h, w_ref[:, c * PROJ_CHUNK:(c + 1) * PROJ_CHUNK])
                for j in range(PROJ_CHUNK // MXU_COLS):
                    lo = c * PROJ_CHUNK + j * MXU_COLS
                    epilogue(acc[:, j * MXU_COLS:(j + 1) * MXU_COLS], rs, slice(lo, lo + MXU_COLS))

    def qk_norm(acc, gain):
        first = lax.broadcasted_iota(jnp.int32, (acc.shape[0], LANES), 1) < HEAD_DIM
        heads = []
        for j in range(acc.shape[1] // LANES):
            blk = acc[:, j * LANES:(j + 1) * LANES]
            sq = blk * blk
            lo = jnp.sum(jnp.where(first, sq, 0.0), axis=-1, keepdims=True)
            hi = jnp.sum(sq, axis=-1, keepdims=True) - lo
            heads.append(blk * lax.rsqrt(jnp.where(first, lo, hi) * (1.0 / HEAD_DIM) + EPS))
        return (jnp.concatenate(heads, axis=1) * gain).astype(BF16)

    @pl.when(n == 0)
    def _():
        def q_out(acc, rs, cs):
            o_ref[rs, cs] = qk_norm(acc, qg_ref[:, cs] * (SCALE * LOG2E))
        per_chunk(q_out)

    @pl.when(n == 1)
    def _():
        lane = lax.broadcasted_iota(jnp.int32, (PROJ_CHUNK_ROWS, MXU_COLS), 1) & (LANES - 1)
        aug1 = jnp.where((lane >> 1) == HEAD_DIM // 2, 1.0, 0.0).astype(BF16)
        aug2 = jnp.where(lane < 2, 1.0, 0.0).astype(BF16)

        def k_out(acc, rs, cs):
            kb = qk_norm(acc, kg_ref[:, cs])
            o_ref[rs, cs] = jnp.where(lane < HEAD_DIM, kb, aug1)
            k2_ref[rs, cs] = jnp.where(lane < HEAD_DIM, aug2, kb)
        per_chunk(k_out)

    @pl.when(n == 2)
    def _():
        def v_out(acc, rs, cs):
            vt_ref[cs, rs] = acc.T.astype(BF16)
        per_chunk(v_out)

    @pl.when(n == 3)
    def _():
        def silu_out(acc, rs, cs):
            o_ref[rs, cs] = (acc * _sigmoid(acc)).astype(BF16)
        per_chunk(silu_out)

    @pl.when((n == 4) | (n == 6))
    def _():
        def keep(acc, rs, cs):
            keep_scr[rs, cs] = acc.astype(BF16)
        per_chunk(keep)

    @pl.when(n == 5)
    def _():
        def conv_in(acc, rs, cs):
            o_ref[rs, cs] = (acc * keep_scr[rs, cs].astype(F32)).astype(BF16)
        per_chunk(conv_in)

    @pl.when(n == 7)
    def _():
        def gate_out(acc, rs, cs):
            o_ref[rs, cs] = (keep_scr[rs, cs].astype(F32) * (acc * _sigmoid(acc))).astype(BF16)
        per_chunk(gate_out)

    @pl.when(n >= 8)
    def _():
        def sigmoid_out(acc, rs, cs):
            o_ref[rs, cs] = _sigmoid(acc).astype(BF16)
        per_chunk(sigmoid_out)


def _proj(x2d, norm_g, w_in, qg, kg):
    t = x2d.shape[0]
    tm = PROJ_ROWS

    def out_map(m, n):
        skipped = (n >= 3).astype(jnp.int32) + (n >= 5).astype(jnp.int32) + (n >= 7).astype(jnp.int32)
        return m, n - skipped

    return pl.pallas_call(
        _proj_kernel,
        out_shape=(jax.ShapeDtypeStruct((t, len(OUT_GROUPS) * D_MODEL), BF16),
                   jax.ShapeDtypeStruct((t, D_MODEL), BF16),
                   jax.ShapeDtypeStruct((D_MODEL, t), BF16)),
        grid=(t // tm, N_COLS),
        in_specs=[
            pl.BlockSpec((tm, D_MODEL), lambda m, n: (m, 0)),
            pl.BlockSpec((1, D_MODEL), lambda m, n: (0, 0)),
            pl.BlockSpec((D_MODEL, D_MODEL), lambda m, n: (0, n)),
            pl.BlockSpec((1, D_MODEL), lambda m, n: (0, 0)),
            pl.BlockSpec((1, D_MODEL), lambda m, n: (0, 0)),
        ],
        out_specs=(pl.BlockSpec((tm, D_MODEL), out_map),
                   pl.BlockSpec((tm, D_MODEL), lambda m, n: (m, 0)),
                   pl.BlockSpec((D_MODEL, tm), lambda m, n: (0, m))),
        scratch_shapes=[pltpu.VMEM((tm, D_MODEL), BF16), pltpu.VMEM((tm, D_MODEL), BF16)],
        compiler_params=pltpu.CompilerParams(
            dimension_semantics=("arbitrary", "arbitrary"), vmem_limit_bytes=VMEM_LIMIT),
        name="proj",
    )(x2d, norm_g, w_in, qg, kg)


def _attn_kernel(rb_ref, q_ref, k1_ref, k2_ref, vt_ref, dl_ref, lq1_ref, lk1_ref, lq2_ref, lk2_ref,
                 sg_ref, o_ref, s_scr, p_scr, q_scr, m_scr, c_scr, *, lam_init, nq, n_pairs):
    t = ATT_TILE
    step = pl.program_id(0)

    @pl.when(step == 0)
    def _():
        s_scr[...] = jnp.zeros(s_scr.shape, BF16)
        p_scr[...] = jnp.zeros(p_scr.shape, BF16)
        m_scr[...] = jnp.zeros(m_scr.shape, BF16)
        c_scr[...] = jnp.zeros(c_scr.shape, F32)

    pair_a = jnp.minimum(step, n_pairs - 1)
    pair_c = jnp.maximum(step - 2, 0)
    ha = lax.rem(pair_a // (nq // ATT_PAIR), N_HEADS)
    lam = (jnp.exp(jnp.sum(lq1_ref[...] * lk1_ref[...], axis=-1, keepdims=True))
           - jnp.exp(jnp.sum(lq2_ref[...] * lk2_ref[...], axis=-1, keepdims=True))
           + lam_init)

    def slabs(x, rows):
        return x.reshape(x.shape[0] // rows, rows, x.shape[1])

    def key_tile(i, r):
        kt = i + r
        kt = jnp.where(kt >= nq, kt - nq, kt)
        return pl.ds(pl.multiple_of(kt * t, t), t)

    def prepare(u):
        ia = lax.rem(pair_a * ATT_PAIR + u, nq)
        ic = lax.rem(pair_c * ATT_PAIR + u, nq)

        q = q_ref[u * t:(u + 1) * t, :]
        lane = lax.broadcasted_iota(jnp.int32, (t, LANES), 1)
        for side, bucket in enumerate((FAR_NEG_BUCKET, FAR_POS_BUCKET)):
            cv = jnp.full((t, LANES), rb_ref[bucket, ha] * LOG2E, F32)
            hi = cv.astype(BF16).astype(F32)
            for comp in range(2):
                base = HEAD_DIM if comp == 0 else 0
                keep = (lane < HEAD_DIM) if comp == 0 else (lane >= HEAD_DIM)
                aug = jnp.where(lane == base, hi, jnp.where(lane == base + 1, cv - hi, 0.0))
                q_scr[u, comp, side] = jnp.where(keep, q, aug.astype(BF16))

        mb1 = m_scr[u, 0]
        mb2 = m_scr[u, 1]
        c1 = c_scr[u, 0]
        rho = jnp.concatenate([c_scr[u, 1], c_scr[u, 1]], axis=0).astype(BF16)

        m0 = jnp.full((F32_ROWS, t), -jnp.inf, F32)
        l0 = jnp.zeros((BF16_ROWS, t), F32)
        return (ia, ic, mb1, mb2, c1, rho), (m0, m0, l0, l0, jnp.zeros((V_DIM, t), F32))

    def tile(u, r, consts, carry):
        ia, ic, mb1, mb2, _, rho = consts
        m1, m2, l1, l2, acc = carry
        a = slabs(p_scr[u, 0, r], BF16_ROWS) - slabs(p_scr[u, 1, r], BF16_ROWS) * rho
        acc = acc + _dot(vt_ref[:, key_tile(ic, r)], a.reshape(t, t))

        def probs(comp, mb):
            p = jnp.exp2((slabs(s_scr[u, comp, r], BF16_ROWS) - mb).reshape(t, t))
            p_scr[u, comp, r] = p
            parts = [p[i * BF16_ROWS:(i + 1) * BF16_ROWS, :] for i in range(t // BF16_ROWS)]
            while len(parts) > 1:
                parts = [parts[i] + parts[i + 1] for i in range(0, len(parts), 2)]
            return parts[0].astype(F32)

        l1 = l1 + probs(0, mb1)
        l2 = l2 + probs(1, mb2)
        wrapped = ia + r >= nq
        side = 1 if r == 0 else jnp.where(wrapped, 0, 1)
        keys = key_tile(ia, r)
        s1 = _dot_nt(k1_ref[keys, :], q_scr[u, 0, side])
        s2 = _dot_nt(k2_ref[keys, :], q_scr[u, 1, side])
        if r in (0, 1, nq - 1):
            offset = (r + 1) % nq - 1
            dl = dl_ref[0, jnp.where(wrapped == (r == nq - 1), offset + 1, 3)]
            s1 = s1 + dl
            s2 = s2 + dl
        s_scr[u, 0, r] = s1.astype(BF16)
        s_scr[u, 1, r] = s2.astype(BF16)
        m1 = jnp.maximum(m1, jnp.max(slabs(s1, F32_ROWS), axis=0))
        m2 = jnp.maximum(m2, jnp.max(slabs(s2, F32_ROWS), axis=0))
        return m1, m2, l1, l2, acc

    def finish(u, consts, carry):
        c1 = consts[4]
        m1, m2, l1, l2, acc = carry
        for comp, m in enumerate((m1, m2)):
            m_scr[u, comp] = jnp.broadcast_to(
                jnp.max(m, axis=0, keepdims=True), (BF16_ROWS, t)).astype(BF16)

        l1 = jnp.sum(l1, axis=0, keepdims=True)
        l2 = jnp.sum(l2, axis=0, keepdims=True)
        c_scr[u, 0] = jnp.broadcast_to(1.0 / l1, (F32_ROWS, t))
        c_scr[u, 1] = jnp.broadcast_to(lam * l1 / l2, (F32_ROWS, t))

        ot = (slabs(acc, F32_ROWS) * c1).reshape(V_DIM, t)
        ms = jnp.mean(ot * ot, axis=0, keepdims=True)
        ot = ot * lax.rsqrt(ms + EPS) * sg_ref[...] * (1.0 - lam_init)
        o_ref[u * t:(u + 1) * t, :] = ot.T.astype(BF16)

    blocks = [prepare(u) for u in range(ATT_PAIR)]
    for u, (consts, carry) in enumerate(blocks):
        for r in range(nq):
            carry = tile(u, r, consts, carry)
        finish(u, consts, carry)


def _attn(rel_bias, proj, k2a, vt, bias_tiles, lq1, lk1, lq2, lk2, subln_g, *, batch, seq, lam_init):
    t = ATT_TILE
    nq = seq // t
    assert nq >= 3 and nq % ATT_PAIR == 0
    ppq = nq // ATT_PAIR
    n_pairs = batch * N_HEADS * ppq

    def split(pair):
        bh = pair // ppq
        return bh // N_HEADS, lax.rem(bh, N_HEADS), lax.rem(pair, ppq)

    def a_pair(s):
        return split(jnp.minimum(s, n_pairs - 1))

    def c_pair(s):
        return split(jnp.maximum(s - 2, 0))

    def q_map(s):
        b, h, j = a_pair(s)
        return b * ppq + j, h

    def k1_map(s):
        b, h, _ = a_pair(s)
        return b, N_HEADS + h

    def k2_map(s):
        b, h, _ = a_pair(s)
        return b, h

    def v_map(s):
        b, h, _ = c_pair(s)
        return h, b

    def dl_map(s):
        return a_pair(s)[1], 0, 0, 0

    def o_map(s):
        b, h, j = c_pair(s)
        return b * ppq + j, h

    small = lambda shape: pl.BlockSpec(shape, lambda s: (0, 0))
    return pl.pallas_call(
        functools.partial(_attn_kernel, lam_init=lam_init, nq=nq, n_pairs=n_pairs),
        out_shape=jax.ShapeDtypeStruct((batch * seq, N_HEADS * V_DIM), BF16),
        grid=(n_pairs + 2,),
        in_specs=[
            pl.BlockSpec(memory_space=pltpu.SMEM),
            pl.BlockSpec((ATT_PAIR * t, LANES), q_map),
            pl.BlockSpec((seq, LANES), k1_map),
            pl.BlockSpec((seq, LANES), k2_map),
            pl.BlockSpec((V_DIM, seq), v_map),
            pl.BlockSpec((1, 4, t, t), dl_map),
            small((1, HEAD_DIM)), small((1, HEAD_DIM)), small((1, HEAD_DIM)), small((1, HEAD_DIM)),
            small((V_DIM, 1)),
        ],
        out_specs=pl.BlockSpec((ATT_PAIR * t, V_DIM), o_map),
        scratch_shapes=[
            pltpu.VMEM((ATT_PAIR, 2, nq, t, t), BF16),
            pltpu.VMEM((ATT_PAIR, 2, nq, t, t), BF16),
            pltpu.VMEM((ATT_PAIR, 2, 2, t, LANES), BF16),
            pltpu.VMEM((ATT_PAIR, 2, BF16_ROWS, t), BF16),
            pltpu.VMEM((ATT_PAIR, 2, F32_ROWS, t), F32),
        ],
        compiler_params=pltpu.CompilerParams(
            dimension_semantics=("arbitrary",), vmem_limit_bytes=VMEM_LIMIT),
        name="attn",
    )(rel_bias, proj, proj, k2a, vt, bias_tiles, lq1, lk1, lq2, lk2, subln_g)


def _post_kernel(o_ref, sga_ref, pc_ref, bg_ref, sma_ref, smb_ref, pcp_ref, pcn_ref, x_ref, cw_ref,
                 wa_ref, wc_ref, wo_ref, out_ref, *, tiles_per_seq):
    tm = out_ref.shape[0]
    pos = lax.rem(pl.program_id(0), tiles_per_seq)

    ua = (o_ref[...].astype(F32) * sga_ref[...].astype(F32)).astype(BF16)
    ya = _dot(ua, wa_ref[...])

    pc = pc_ref[...].astype(F32)
    prev_row = pcp_ref[...].astype(F32)[BF16_ROWS - 1:BF16_ROWS, :]
    next_row = pcn_ref[...].astype(F32)[0:1, :]
    prev_row = jnp.where(pos == 0, 0.0, prev_row)
    next_row = jnp.where(pos == tiles_per_seq - 1, 0.0, next_row)
    row = lax.broadcasted_iota(jnp.int32, pc.shape, 0)
    up = jnp.where(row == 0, prev_row, pltpu.roll(pc, 1, axis=0))
    dn = jnp.where(row == tm - 1, next_row, pltpu.roll(pc, tm - 1, axis=0))
    cw = cw_ref[...]
    cv = up * cw[0:1, :] + pc * cw[1:2, :] + dn * cw[2:3, :]
    ub = (bg_ref[...].astype(F32) * cv).astype(BF16)
    yb = _dot(ub, wc_ref[...])

    merged = (sma_ref[...].astype(F32) * ya + smb_ref[...].astype(F32) * yb).astype(BF16)
    out_ref[...] = x_ref[...] + _dot(merged, wo_ref[...])


def _post(o, proj, x2d, conv_w, wa, wc, wo, *, seq):
    t = x2d.shape[0]
    tm = POST_ROWS
    hb = tm // BF16_ROWS
    n_hb = t // BF16_ROWS
    grp = OUT_GROUPS.index
    col = lambda c: pl.BlockSpec((tm, D_MODEL), lambda m: (m, c))
    prev = lambda c: pl.BlockSpec((BF16_ROWS, D_MODEL), lambda m: (jnp.maximum(m * hb - 1, 0), c))
    nxt = lambda c: pl.BlockSpec((BF16_ROWS, D_MODEL),
                                 lambda m: (jnp.minimum((m + 1) * hb, n_hb - 1), c))
    whole = lambda r: pl.BlockSpec((r, D_MODEL), lambda m: (0, 0))
    return pl.pallas_call(
        functools.partial(_post_kernel, tiles_per_seq=seq // tm),
        out_shape=jax.ShapeDtypeStruct((t, D_MODEL), F32),
        grid=(t // tm,),
        in_specs=[
            col(0),
            col(grp("silu_ga")), col(grp("c_xc")), col(grp("b_silu_gb")),
            col(grp("sig_ma")), col(grp("sig_mb")),
            prev(grp("c_xc")), nxt(grp("c_xc")),
            col(0),
            whole(3), whole(D_MODEL), whole(D_MODEL), whole(D_MODEL),
        ],
        out_specs=pl.BlockSpec((tm, D_MODEL), lambda m: (m, 0)),
        compiler_params=pltpu.CompilerParams(
            dimension_semantics=("arbitrary",), vmem_limit_bytes=VMEM_LIMIT),
        name="post",
    )(o, proj, proj, proj, proj, proj, proj, proj, x2d, conv_w, wa, wc, wo)


def _lambda_init(layer_idx):
    return 0.8 - 0.6 * math.exp(-0.3 * layer_idx)


def _trunk(x, bias_tiles, p):
    batch, seq, _ = x.shape
    x2d = x.reshape(batch * seq, D_MODEL)
    for l in range(DEPTH):
        proj, k2a, vt = _proj(x2d, p["norm_g"][l], p["w_in"][l], p["qg"][l], p["kg"][l])
        o = _attn(p["rel_bias"], proj, k2a, vt, bias_tiles, p["lq1"][l], p["lk1"][l], p["lq2"][l],
                  p["lk2"][l], p["subln_g"][l], batch=batch, seq=seq, lam_init=_lambda_init(l))
        x2d = _post(o, proj, x2d, p["conv_w"][l], p["wa"][l], p["wc"][l], p["wo"][l], seq=seq)
    return x2d.reshape(batch, seq, D_MODEL)


def kernel(x_prompt, x_sample, rel_bias, norm_g, w_in, q_norm_g, k_norm_g, lambda_q1, lambda_k1,
           lambda_q2, lambda_k2, subln_g, w_attn_out, conv_w, w_conv_out, w_o):
    reps = D_MODEL // HEAD_DIM
    p = {
        "rel_bias": rel_bias,
        "norm_g": norm_g.reshape(DEPTH, 1, D_MODEL),
        "w_in": w_in.astype(BF16),
        "qg": jnp.tile(q_norm_g, (1, reps)).reshape(DEPTH, 1, D_MODEL),
        "kg": jnp.tile(k_norm_g, (1, reps)).reshape(DEPTH, 1, D_MODEL),
        "lq1": lambda_q1.reshape(DEPTH, 1, HEAD_DIM),
        "lk1": lambda_k1.reshape(DEPTH, 1, HEAD_DIM),
        "lq2": lambda_q2.reshape(DEPTH, 1, HEAD_DIM),
        "lk2": lambda_k2.reshape(DEPTH, 1, HEAD_DIM),
        "subln_g": subln_g.reshape(DEPTH, V_DIM, 1),
        "conv_w": conv_w,
        "wa": w_attn_out.astype(BF16),
        "wc": w_conv_out.astype(BF16),
        "wo": w_o.astype(BF16),
    }
    bias_tiles = _bias_tiles(rel_bias)
    return (_trunk(x_prompt, bias_tiles, p), _trunk(x_sample, bias_tiles, p))
```

```python
import functools
import math

import jax
import jax.numpy as jnp
from jax import lax
from jax.experimental import pallas as pl
from jax.experimental.pallas import tpu as pltpu

D_MODEL = 1024
DEPTH = 4
N_HEADS = 8
HEAD_DIM = 64
V_DIM = 2 * HEAD_DIM
SCALE = HEAD_DIM ** -0.5
NUM_BUCKETS = 32
MAX_DISTANCE = 128
N_COLS = 10
OUT_GROUPS = ("q", "k1", "silu_ga", "c_xc", "b_silu_gb", "sig_ma", "sig_mb")
EPS = 1e-6
LOG2E = 1.4426950408889634
FAR_NEG_BUCKET = NUM_BUCKETS // 2 - 1
FAR_POS_BUCKET = NUM_BUCKETS - 1

LANES = 128
BF16_ROWS = 16
F32_ROWS = 8
MXU_COLS = 256
PROJ_CHUNK = 2 * MXU_COLS
PROJ_CHUNK_ROWS = 512
ATT_TILE = 512
PROJ_ROWS = 1024
POST_ROWS = 512
ATT_PAIR = 2
VMEM_LIMIT = 56 * 1024 * 1024

F32 = jnp.float32
BF16 = jnp.bfloat16


def _dot(a, b):
    return jnp.dot(a, b, preferred_element_type=F32)


def _sigmoid(x):
    return 0.5 * jnp.tanh(0.5 * x) + 0.5


def _dot_nt(a, b):
    return lax.dot_general(a, b, (((1,), (1,)), ((), ())), preferred_element_type=F32)


def _rel_bucket(rel):
    nb = NUM_BUCKETS // 2
    ret = (rel > 0).astype(jnp.int32) * nb
    n = jnp.abs(rel)
    max_exact = nb // 2
    is_small = n < max_exact
    nf = jnp.maximum(n, 1).astype(F32)
    large = max_exact + (jnp.log(nf / max_exact) / math.log(MAX_DISTANCE / max_exact)
                         * (nb - max_exact)).astype(jnp.int32)
    large = jnp.minimum(large, nb - 1)
    return ret + jnp.where(is_small, n, large)


def _bias_kernel(rb_ref, out_ref):
    h = pl.program_id(0)
    t = ATT_TILE
    key = lax.broadcasted_iota(jnp.int32, (t, t), 0)
    qry = lax.broadcasted_iota(jnp.int32, (t, t), 1)
    for v in range(3):
        bucket = _rel_bucket(key - qry + (v - 1) * t)
        val = jnp.zeros((t, t), F32)
        for b in range(NUM_BUCKETS):
            val = jnp.where(bucket == b, rb_ref[b, h], val)
        far = rb_ref[FAR_NEG_BUCKET, h] if v == 0 else rb_ref[FAR_POS_BUCKET, h]
        out_ref[0, v] = (val - far) * LOG2E
    out_ref[0, 3] = jnp.zeros((t, t), F32)


def _bias_tiles(rel_bias):
    t = ATT_TILE
    return pl.pallas_call(
        _bias_kernel,
        out_shape=jax.ShapeDtypeStruct((N_HEADS, 4, t, t), F32),
        grid=(N_HEADS,),
        in_specs=[pl.BlockSpec(memory_space=pltpu.SMEM)],
        out_specs=pl.BlockSpec((1, 4, t, t), lambda h: (h, 0, 0, 0)),
        compiler_params=pltpu.CompilerParams(
            dimension_semantics=("arbitrary",), vmem_limit_bytes=VMEM_LIMIT),
        name="bias_tiles",
    )(rel_bias)


def _proj_kernel(x_ref, ng_ref, w_ref, qg_ref, kg_ref, o_ref, k2_ref, vt_ref, h_scr, keep_scr):
    n = pl.program_id(1)

    @pl.when(n == 0)
    def _():
        xf = x_ref[...]
        ms = jnp.mean(xf * xf, axis=-1, keepdims=True)
        h_scr[...] = (xf * lax.rsqrt(ms + EPS) * ng_ref[...]).astype(BF16)

    rows = h_scr.shape[0]

    def per_chunk(epilogue):
        for r0 in range(0, rows, PROJ_CHUNK_ROWS):
            rs = slice(r0, r0 + PROJ_CHUNK_ROWS)
            h = h_scr[rs, :]
            for c in range(D_MODEL // PROJ_CHUNK):
                acc = _dot(h, w_ref[:, c * PROJ_CHUNK:(c + 1) * PROJ_CHUNK])
                for j in range(PROJ_CHUNK // MXU_COLS):
                    lo = c * PROJ_CHUNK + j * MXU_COLS
                    epilogue(acc[:, j * MXU_COLS:(j + 1) * MXU_COLS], rs, slice(lo, lo + MXU_COLS))

    def qk_norm(acc, gain):
        first = lax.broadcasted_iota(jnp.int32, (acc.shape[0], LANES), 1) < HEAD_DIM
        heads = []
        for j in range(acc.shape[1] // LANES):
            blk = acc[:, j * LANES:(j + 1) * LANES]
            sq = blk * blk
            lo = jnp.sum(jnp.where(first, sq, 0.0), axis=-1, keepdims=True)
            hi = jnp.sum(sq, axis=-1, keepdims=True) - lo
            heads.append(blk * lax.rsqrt(jnp.where(first, lo, hi) * (1.0 / HEAD_DIM) + EPS))
        return (jnp.concatenate(heads, axis=1) * gain).astype(BF16)

    @pl.when(n == 0)
    def _():
        def q_out(acc, rs, cs):
            o_ref[rs, cs] = qk_norm(acc, qg_ref[:, cs] * (SCALE * LOG2E))
        per_chunk(q_out)

    @pl.when(n == 1)
    def _():
        lane = lax.broadcasted_iota(jnp.int32, (PROJ_CHUNK_ROWS, MXU_COLS), 1) & (LANES - 1)
        aug1 = jnp.where((lane >> 1) == HEAD_DIM // 2, 1.0, 0.0).astype(BF16)
        aug2 = jnp.where(lane < 2, 1.0, 0.0).astype(BF16)

        def k_out(acc, rs, cs):
            kb = qk_norm(acc, kg_ref[:, cs])
            o_ref[rs, cs] = jnp.where(lane < HEAD_DIM, kb, aug1)
            k2_ref[rs, cs] = jnp.where(lane < HEAD_DIM, aug2, kb)
        per_chunk(k_out)

    @pl.when(n == 2)
    def _():
        def v_out(acc, rs, cs):
            vt_ref[cs, rs] = acc.T.astype(BF16)
        per_chunk(v_out)

    @pl.when(n == 3)
    def _():
        def silu_out(acc, rs, cs):
            o_ref[rs, cs] = (acc * _sigmoid(acc)).astype(BF16)
        per_chunk(silu_out)

    @pl.when((n == 4) | (n == 6))
    def _():
        def keep(acc, rs, cs):
            keep_scr[rs, cs] = acc.astype(BF16)
        per_chunk(keep)

    @pl.when(n == 5)
    def _():
        def conv_in(acc, rs, cs):
            o_ref[rs, cs] = (acc * keep_scr[rs, cs].astype(F32)).astype(BF16)
        per_chunk(conv_in)

    @pl.when(n == 7)
    def _():
        def gate_out(acc, rs, cs):
            o_ref[rs, cs] = (keep_scr[rs, cs].astype(F32) * (acc * _sigmoid(acc))).astype(BF16)
        per_chunk(gate_out)

    @pl.when(n >= 8)
    def _():
        def sigmoid_out(acc, rs, cs):
            o_ref[rs, cs] = _sigmoid(acc).astype(BF16)
        per_chunk(sigmoid_out)


def _proj(x2d, norm_g, w_in, qg, kg, *, layer):
    t = x2d.shape[0]
    tm = PROJ_ROWS

    def out_map(m, n):
        skipped = (n >= 3).astype(jnp.int32) + (n >= 5).astype(jnp.int32) + (n >= 7).astype(jnp.int32)
        return m, n - skipped

    return pl.pallas_call(
        _proj_kernel,
        out_shape=(jax.ShapeDtypeStruct((t, len(OUT_GROUPS) * D_MODEL), BF16),
                   jax.ShapeDtypeStruct((t, D_MODEL), BF16),
                   jax.ShapeDtypeStruct((D_MODEL, t), BF16)),
        grid=(t // tm, N_COLS),
        in_specs=[
            pl.BlockSpec((tm, D_MODEL), lambda m, n: (m, 0)),
            pl.BlockSpec((1, D_MODEL), lambda m, n: (0, 0)),
            pl.BlockSpec((None, D_MODEL, D_MODEL), lambda m, n: (layer, 0, n)),
            pl.BlockSpec((1, D_MODEL), lambda m, n: (0, 0)),
            pl.BlockSpec((1, D_MODEL), lambda m, n: (0, 0)),
        ],
        out_specs=(pl.BlockSpec((tm, D_MODEL), out_map),
                   pl.BlockSpec((tm, D_MODEL), lambda m, n: (m, 0)),
                   pl.BlockSpec((D_MODEL, tm), lambda m, n: (0, m))),
        scratch_shapes=[pltpu.VMEM((tm, D_MODEL), BF16), pltpu.VMEM((tm, D_MODEL), BF16)],
        compiler_params=pltpu.CompilerParams(
            dimension_semantics=("arbitrary", "arbitrary"), vmem_limit_bytes=VMEM_LIMIT),
        name="proj",
    )(x2d, norm_g, w_in, qg, kg)


def _attn_kernel(rb_ref, q_ref, k1_ref, k2_ref, vt_ref, dl_ref, lq1_ref, lk1_ref, lq2_ref, lk2_ref,
                 sg_ref, o_ref, s_scr, p_scr, q_scr, m_scr, c_scr, *, lam_init, nq, n_pairs):
    t = ATT_TILE
    step = pl.program_id(0)

    @pl.when(step == 0)
    def _():
        s_scr[...] = jnp.zeros(s_scr.shape, BF16)
        p_scr[...] = jnp.zeros(p_scr.shape, BF16)
        m_scr[...] = jnp.zeros(m_scr.shape, BF16)
        c_scr[...] = jnp.zeros(c_scr.shape, F32)

    pair_a = jnp.minimum(step, n_pairs - 1)
    pair_c = jnp.maximum(step - 2, 0)
    ha = lax.rem(pair_a // (nq // ATT_PAIR), N_HEADS)
    lam = (jnp.exp(jnp.sum(lq1_ref[...] * lk1_ref[...], axis=-1, keepdims=True))
           - jnp.exp(jnp.sum(lq2_ref[...] * lk2_ref[...], axis=-1, keepdims=True))
           + lam_init)

    def slabs(x, rows):
        return x.reshape(x.shape[0] // rows, rows, x.shape[1])

    def key_tile(i, r):
        kt = i + r
        kt = jnp.where(kt >= nq, kt - nq, kt)
        return pl.ds(pl.multiple_of(kt * t, t), t)

    def prepare(u):
        ia = lax.rem(pair_a * ATT_PAIR + u, nq)
        ic = lax.rem(pair_c * ATT_PAIR + u, nq)

        q = q_ref[u * t:(u + 1) * t, :]
        lane = lax.broadcasted_iota(jnp.int32, (t, LANES), 1)
        for side, bucket in enumerate((FAR_NEG_BUCKET, FAR_POS_BUCKET)):
            cv = jnp.full((t, LANES), rb_ref[bucket, ha] * LOG2E, F32)
            hi = cv.astype(BF16).astype(F32)
            for comp in range(2):
                base = HEAD_DIM if comp == 0 else 0
                keep = (lane < HEAD_DIM) if comp == 0 else (lane >= HEAD_DIM)
                aug = jnp.where(lane == base, hi, jnp.where(lane == base + 1, cv - hi, 0.0))
                q_scr[u, comp, side] = jnp.where(keep, q, aug.astype(BF16))

        mb1 = m_scr[u, 0]
        mb2 = m_scr[u, 1]
        c1 = c_scr[u, 0]
        rho = jnp.concatenate([c_scr[u, 1], c_scr[u, 1]], axis=0).astype(BF16)

        m0 = jnp.full((F32_ROWS, t), -jnp.inf, F32)
        l0 = jnp.zeros((BF16_ROWS, t), F32)
        return (ia, ic, mb1, mb2, c1, rho), (m0, m0, l0, l0, jnp.zeros((V_DIM, t), F32))

    def tile(u, r, consts, carry):
        ia, ic, mb1, mb2, _, rho = consts
        m1, m2, l1, l2, acc = carry
        a = slabs(p_scr[u, 0, r], BF16_ROWS) - slabs(p_scr[u, 1, r], BF16_ROWS) * rho
        acc = acc + _dot(vt_ref[:, key_tile(ic, r)], a.reshape(t, t))

        def probs(comp, mb):
            p = jnp.exp2((slabs(s_scr[u, comp, r], BF16_ROWS) - mb).reshape(t, t))
            p_scr[u, comp, r] = p
            parts = [p[i * BF16_ROWS:(i + 1) * BF16_ROWS, :] for i in range(t // BF16_ROWS)]
            while len(parts) > 1:
                parts = [parts[i] + parts[i + 1] for i in range(0, len(parts), 2)]
            return parts[0].astype(F32)

        l1 = l1 + probs(0, mb1)
        l2 = l2 + probs(1, mb2)
        wrapped = ia + r >= nq
        side = 1 if r == 0 else jnp.where(wrapped, 0, 1)
        keys = key_tile(ia, r)
        s1 = _dot_nt(k1_ref[keys, :], q_scr[u, 0, side])
        s2 = _dot_nt(k2_ref[keys, :], q_scr[u, 1, side])
        if r in (0, 1, nq - 1):
            offset = (r + 1) % nq - 1
            dl = dl_ref[0, jnp.where(wrapped == (r == nq - 1), offset + 1, 3)]
            s1 = s1 + dl
            s2 = s2 + dl
        s_scr[u, 0, r] = s1.astype(BF16)
        s_scr[u, 1, r] = s2.astype(BF16)
        m1 = jnp.maximum(m1, jnp.max(slabs(s1, F32_ROWS), axis=0))
        m2 = jnp.maximum(m2, jnp.max(slabs(s2, F32_ROWS), axis=0))
        return m1, m2, l1, l2, acc

    def finish(u, consts, carry):
        c1 = consts[4]
        m1, m2, l1, l2, acc = carry
        for comp, m in enumerate((m1, m2)):
            m_scr[u, comp] = jnp.broadcast_to(
                jnp.max(m, axis=0, keepdims=True), (BF16_ROWS, t)).astype(BF16)

        l1 = jnp.sum(l1, axis=0, keepdims=True)
        l2 = jnp.sum(l2, axis=0, keepdims=True)
        c_scr[u, 0] = jnp.broadcast_to(1.0 / l1, (F32_ROWS, t))
        c_scr[u, 1] = jnp.broadcast_to(lam * l1 / l2, (F32_ROWS, t))

        ot = (slabs(acc, F32_ROWS) * c1).reshape(V_DIM, t)
        ms = jnp.mean(ot * ot, axis=0, keepdims=True)
        ot = ot * lax.rsqrt(ms + EPS) * sg_ref[...] * (1.0 - lam_init)
        o_ref[u * t:(u + 1) * t, :] = ot.T.astype(BF16)

    blocks = [prepare(u) for u in range(ATT_PAIR)]
    for u, (consts, carry) in enumerate(blocks):
        for r in range(nq):
            carry = tile(u, r, consts, carry)
        finish(u, consts, carry)


def _attn(rel_bias, proj, k2a, vt, bias_tiles, lq1, lk1, lq2, lk2, subln_g, *, batch, seq, lam_init):
    t = ATT_TILE
    nq = seq // t
    assert nq >= 3 and nq % ATT_PAIR == 0
    ppq = nq // ATT_PAIR
    n_pairs = batch * N_HEADS * ppq

    def split(pair):
        bh = pair // ppq
        return bh // N_HEADS, lax.rem(bh, N_HEADS), lax.rem(pair, ppq)

    def a_pair(s):
        return split(jnp.minimum(s, n_pairs - 1))

    def c_pair(s):
        return split(jnp.maximum(s - 2, 0))

    def q_map(s):
        b, h, j = a_pair(s)
        return b * ppq + j, h

    def k1_map(s):
        b, h, _ = a_pair(s)
        return b, N_HEADS + h

    def k2_map(s):
        b, h, _ = a_pair(s)
        return b, h

    def v_map(s):
        b, h, _ = c_pair(s)
        return h, b

    def dl_map(s):
        return a_pair(s)[1], 0, 0, 0

    def o_map(s):
        b, h, j = c_pair(s)
        return b * ppq + j, h

    small = lambda shape: pl.BlockSpec(shape, lambda s: (0, 0))
    return pl.pallas_call(
        functools.partial(_attn_kernel, lam_init=lam_init, nq=nq, n_pairs=n_pairs),
        out_shape=jax.ShapeDtypeStruct((batch * seq, N_HEADS * V_DIM), BF16),
        grid=(n_pairs + 2,),
        in_specs=[
            pl.BlockSpec(memory_space=pltpu.SMEM),
            pl.BlockSpec((ATT_PAIR * t, LANES), q_map),
            pl.BlockSpec((seq, LANES), k1_map),
            pl.BlockSpec((seq, LANES), k2_map),
            pl.BlockSpec((V_DIM, seq), v_map),
            pl.BlockSpec((1, 4, t, t), dl_map),
            small((1, HEAD_DIM)), small((1, HEAD_DIM)), small((1, HEAD_DIM)), small((1, HEAD_DIM)),
            small((V_DIM, 1)),
        ],
        out_specs=pl.BlockSpec((ATT_PAIR * t, V_DIM), o_map),
        scratch_shapes=[
            pltpu.VMEM((ATT_PAIR, 2, nq, t, t), BF16),
            pltpu.VMEM((ATT_PAIR, 2, nq, t, t), BF16),
            pltpu.VMEM((ATT_PAIR, 2, 2, t, LANES), BF16),
            pltpu.VMEM((ATT_PAIR, 2, BF16_ROWS, t), BF16),
            pltpu.VMEM((ATT_PAIR, 2, F32_ROWS, t), F32),
        ],
        compiler_params=pltpu.CompilerParams(
            dimension_semantics=("arbitrary",), vmem_limit_bytes=VMEM_LIMIT),
        name="attn",
    )(rel_bias, proj, proj, k2a, vt, bias_tiles, lq1, lk1, lq2, lk2, subln_g)


def _post_kernel(o_ref, sga_ref, pc_ref, bg_ref, sma_ref, smb_ref, pcp_ref, pcn_ref, x_ref, cw_ref,
                 wa_ref, wc_ref, wo_ref, out_ref, *, tiles_per_seq):
    tm = out_ref.shape[0]
    pos = lax.rem(pl.program_id(0), tiles_per_seq)

    ua = (o_ref[...].astype(F32) * sga_ref[...].astype(F32)).astype(BF16)
    ya = _dot(ua, wa_ref[...])

    pc = pc_ref[...].astype(F32)
    prev_row = pcp_ref[...].astype(F32)[BF16_ROWS - 1:BF16_ROWS, :]
    next_row = pcn_ref[...].astype(F32)[0:1, :]
    prev_row = jnp.where(pos == 0, 0.0, prev_row)
    next_row = jnp.where(pos == tiles_per_seq - 1, 0.0, next_row)
    row = lax.broadcasted_iota(jnp.int32, pc.shape, 0)
    up = jnp.where(row == 0, prev_row, pltpu.roll(pc, 1, axis=0))
    dn = jnp.where(row == tm - 1, next_row, pltpu.roll(pc, tm - 1, axis=0))
    cw = cw_ref[...]
    cv = up * cw[0:1, :] + pc * cw[1:2, :] + dn * cw[2:3, :]
    ub = (bg_ref[...].astype(F32) * cv).astype(BF16)
    yb = _dot(ub, wc_ref[...])

    merged = (sma_ref[...].astype(F32) * ya + smb_ref[...].astype(F32) * yb).astype(BF16)
    out_ref[...] = x_ref[...] + _dot(merged, wo_ref[...])


def _post(o, proj, x2d, conv_w, wa, wc, wo, *, seq):
    t = x2d.shape[0]
    tm = POST_ROWS
    hb = tm // BF16_ROWS
    n_hb = t // BF16_ROWS
    grp = OUT_GROUPS.index
    col = lambda c: pl.BlockSpec((tm, D_MODEL), lambda m: (m, c))
    prev = lambda c: pl.BlockSpec((BF16_ROWS, D_MODEL), lambda m: (jnp.maximum(m * hb - 1, 0), c))
    nxt = lambda c: pl.BlockSpec((BF16_ROWS, D_MODEL),
                                 lambda m: (jnp.minimum((m + 1) * hb, n_hb - 1), c))
    whole = lambda r: pl.BlockSpec((r, D_MODEL), lambda m: (0, 0))
    return pl.pallas_call(
        functools.partial(_post_kernel, tiles_per_seq=seq // tm),
        out_shape=jax.ShapeDtypeStruct((t, D_MODEL), F32),
        grid=(t // tm,),
        in_specs=[
            col(0),
            col(grp("silu_ga")), col(grp("c_xc")), col(grp("b_silu_gb")),
            col(grp("sig_ma")), col(grp("sig_mb")),
            prev(grp("c_xc")), nxt(grp("c_xc")),
            col(0),
            whole(3), whole(D_MODEL), whole(D_MODEL), whole(D_MODEL),
        ],
        out_specs=pl.BlockSpec((tm, D_MODEL), lambda m: (m, 0)),
        compiler_params=pltpu.CompilerParams(
            dimension_semantics=("arbitrary",), vmem_limit_bytes=VMEM_LIMIT),
        name="post",
    )(o, proj, proj, proj, proj, proj, proj, proj, x2d, conv_w, wa, wc, wo)


def _lambda_init(layer_idx):
    return 0.8 - 0.6 * math.exp(-0.3 * layer_idx)


def _trunk(x, bias_tiles, p):
    batch, seq, _ = x.shape
    x2d = x.reshape(batch * seq, D_MODEL)
    for l in range(DEPTH):
        proj, k2a, vt = _proj(x2d, p["norm_g"][l], p["w_in"], p["qg"][l], p["kg"][l], layer=l)
        o = _attn(p["rel_bias"], proj, k2a, vt, bias_tiles, p["lq1"][l], p["lk1"][l], p["lq2"][l],
                  p["lk2"][l], p["subln_g"][l], batch=batch, seq=seq, lam_init=_lambda_init(l))
        x2d = _post(o, proj, x2d, p["conv_w"][l], p["wa"][l], p["wc"][l], p["wo"][l], seq=seq)
    return x2d.reshape(batch, seq, D_MODEL)


def kernel(x_prompt, x_sample, rel_bias, norm_g, w_in, q_norm_g, k_norm_g, lambda_q1, lambda_k1,
           lambda_q2, lambda_k2, subln_g, w_attn_out, conv_w, w_conv_out, w_o):
    reps = D_MODEL // HEAD_DIM
    p = {
        "rel_bias": rel_bias,
        "norm_g": norm_g.reshape(DEPTH, 1, D_MODEL),
        "w_in": w_in.astype(BF16),
        "qg": jnp.tile(q_norm_g, (1, reps)).reshape(DEPTH, 1, D_MODEL),
        "kg": jnp.tile(k_norm_g, (1, reps)).reshape(DEPTH, 1, D_MODEL),
        "lq1": lambda_q1.reshape(DEPTH, 1, HEAD_DIM),
        "lk1": lambda_k1.reshape(DEPTH, 1, HEAD_DIM),
        "lq2": lambda_q2.reshape(DEPTH, 1, HEAD_DIM),
        "lk2": lambda_k2.reshape(DEPTH, 1, HEAD_DIM),
        "subln_g": subln_g.reshape(DEPTH, V_DIM, 1),
        "conv_w": conv_w,
        "wa": w_attn_out.astype(BF16),
        "wc": w_conv_out.astype(BF16),
        "wo": w_o.astype(BF16),
    }
    bias_tiles = _bias_tiles(rel_bias)
    return (_trunk(x_prompt, bias_tiles, p), _trunk(x_sample, bias_tiles, p))
```

```python
import functools
import math

import jax
import jax.numpy as jnp
from jax import lax
from jax.experimental import pallas as pl
from jax.experimental.pallas import tpu as pltpu

D_MODEL = 1024
DEPTH = 4
N_HEADS = 8
HEAD_DIM = 64
V_DIM = 2 * HEAD_DIM
SCALE = HEAD_DIM ** -0.5
NUM_BUCKETS = 32
MAX_DISTANCE = 128
N_COLS = 10
OUT_GROUPS = ("q", "k1", "silu_ga", "c_xc", "b_silu_gb", "sig_ma", "sig_mb")
EPS = 1e-6
LOG2E = 1.4426950408889634
FAR_NEG_BUCKET = NUM_BUCKETS // 2 - 1
FAR_POS_BUCKET = NUM_BUCKETS - 1

LANES = 128
BF16_ROWS = 16
F32_ROWS = 8
MXU_COLS = 256
PROJ_CHUNK = 2 * MXU_COLS
PROJ_CHUNK_ROWS = 512
ATT_TILE = 512
PROJ_ROWS = 1024
POST_ROWS = 512
POST_CHUNK_ROWS = 256
ATT_PAIR = 2
VMEM_LIMIT = 56 * 1024 * 1024

F32 = jnp.float32
BF16 = jnp.bfloat16


def _dot(a, b):
    return jnp.dot(a, b, preferred_element_type=F32)


def _sigmoid(x):
    return 0.5 * jnp.tanh(0.5 * x) + 0.5


def _dot_nt(a, b):
    return lax.dot_general(a, b, (((1,), (1,)), ((), ())), preferred_element_type=F32)


def _rel_bucket(rel):
    nb = NUM_BUCKETS // 2
    ret = (rel > 0).astype(jnp.int32) * nb
    n = jnp.abs(rel)
    max_exact = nb // 2
    is_small = n < max_exact
    nf = jnp.maximum(n, 1).astype(F32)
    large = max_exact + (jnp.log(nf / max_exact) / math.log(MAX_DISTANCE / max_exact)
                         * (nb - max_exact)).astype(jnp.int32)
    large = jnp.minimum(large, nb - 1)
    return ret + jnp.where(is_small, n, large)


def _bias_kernel(rb_ref, out_ref):
    h = pl.program_id(0)
    t = ATT_TILE
    key = lax.broadcasted_iota(jnp.int32, (t, t), 0)
    qry = lax.broadcasted_iota(jnp.int32, (t, t), 1)
    for v in range(3):
        bucket = _rel_bucket(key - qry + (v - 1) * t)
        val = jnp.zeros((t, t), F32)
        for b in range(NUM_BUCKETS):
            val = jnp.where(bucket == b, rb_ref[b, h], val)
        far = rb_ref[FAR_NEG_BUCKET, h] if v == 0 else rb_ref[FAR_POS_BUCKET, h]
        out_ref[0, v] = (val - far) * LOG2E
    out_ref[0, 3] = jnp.zeros((t, t), F32)


def _bias_tiles(rel_bias):
    t = ATT_TILE
    return pl.pallas_call(
        _bias_kernel,
        out_shape=jax.ShapeDtypeStruct((N_HEADS, 4, t, t), F32),
        grid=(N_HEADS,),
        in_specs=[pl.BlockSpec(memory_space=pltpu.SMEM)],
        out_specs=pl.BlockSpec((1, 4, t, t), lambda h: (h, 0, 0, 0)),
        compiler_params=pltpu.CompilerParams(
            dimension_semantics=("arbitrary",), vmem_limit_bytes=VMEM_LIMIT),
        name="bias_tiles",
    )(rel_bias)


def _proj_kernel(x_ref, ng_ref, w_ref, qg_ref, kg_ref, o_ref, k2_ref, vt_ref, h_scr, keep_scr):
    n = pl.program_id(1)

    @pl.when(n == 0)
    def _():
        xf = x_ref[...]
        ms = jnp.mean(xf * xf, axis=-1, keepdims=True)
        h_scr[...] = (xf * lax.rsqrt(ms + EPS) * ng_ref[...]).astype(BF16)

    rows = h_scr.shape[0]

    def per_chunk(epilogue):
        for r0 in range(0, rows, PROJ_CHUNK_ROWS):
            rs = slice(r0, r0 + PROJ_CHUNK_ROWS)
            h = h_scr[rs, :]
            for c in range(D_MODEL // PROJ_CHUNK):
                acc = _dot(h, w_ref[:, c * PROJ_CHUNK:(c + 1) * PROJ_CHUNK])
                for j in range(PROJ_CHUNK // MXU_COLS):
                    lo = c * PROJ_CHUNK + j * MXU_COLS
                    epilogue(acc[:, j * MXU_COLS:(j + 1) * MXU_COLS], rs, slice(lo, lo + MXU_COLS))

    def qk_norm(acc, gain):
        first = lax.broadcasted_iota(jnp.int32, (acc.shape[0], LANES), 1) < HEAD_DIM
        heads = []
        for j in range(acc.shape[1] // LANES):
            blk = acc[:, j * LANES:(j + 1) * LANES]
            sq = blk * blk
            lo = jnp.sum(jnp.where(first, sq, 0.0), axis=-1, keepdims=True)
            hi = jnp.sum(sq, axis=-1, keepdims=True) - lo
            heads.append(blk * lax.rsqrt(jnp.where(first, lo, hi) * (1.0 / HEAD_DIM) + EPS))
        return (jnp.concatenate(heads, axis=1) * gain).astype(BF16)

    @pl.when(n == 0)
    def _():
        def q_out(acc, rs, cs):
            o_ref[rs, cs] = qk_norm(acc, qg_ref[:, cs] * (SCALE * LOG2E))
        per_chunk(q_out)

    @pl.when(n == 1)
    def _():
        lane = lax.broadcasted_iota(jnp.int32, (PROJ_CHUNK_ROWS, MXU_COLS), 1) & (LANES - 1)
        aug1 = jnp.where((lane >> 1) == HEAD_DIM // 2, 1.0, 0.0).astype(BF16)
        aug2 = jnp.where(lane < 2, 1.0, 0.0).astype(BF16)

        def k_out(acc, rs, cs):
            kb = qk_norm(acc, kg_ref[:, cs])
            o_ref[rs, cs] = jnp.where(lane < HEAD_DIM, kb, aug1)
            k2_ref[rs, cs] = jnp.where(lane < HEAD_DIM, aug2, kb)
        per_chunk(k_out)

    @pl.when(n == 2)
    def _():
        def v_out(acc, rs, cs):
            vt_ref[cs, rs] = acc.T.astype(BF16)
        per_chunk(v_out)

    @pl.when(n == 3)
    def _():
        def silu_out(acc, rs, cs):
            o_ref[rs, cs] = (acc * _sigmoid(acc)).astype(BF16)
        per_chunk(silu_out)

    @pl.when((n == 4) | (n == 6))
    def _():
        def keep(acc, rs, cs):
            keep_scr[rs, cs] = acc.astype(BF16)
        per_chunk(keep)

    @pl.when(n == 5)
    def _():
        def conv_in(acc, rs, cs):
            o_ref[rs, cs] = (acc * keep_scr[rs, cs].astype(F32)).astype(BF16)
        per_chunk(conv_in)

    @pl.when(n == 7)
    def _():
        def gate_out(acc, rs, cs):
            o_ref[rs, cs] = (keep_scr[rs, cs].astype(F32) * (acc * _sigmoid(acc))).astype(BF16)
        per_chunk(gate_out)

    @pl.when(n >= 8)
    def _():
        def sigmoid_out(acc, rs, cs):
            o_ref[rs, cs] = _sigmoid(acc).astype(BF16)
        per_chunk(sigmoid_out)


def _proj(x2d, norm_g, w_in, qg, kg):
    t = x2d.shape[0]
    tm = PROJ_ROWS

    def out_map(m, n):
        skipped = (n >= 3).astype(jnp.int32) + (n >= 5).astype(jnp.int32) + (n >= 7).astype(jnp.int32)
        return m, n - skipped

    return pl.pallas_call(
        _proj_kernel,
        out_shape=(jax.ShapeDtypeStruct((t, len(OUT_GROUPS) * D_MODEL), BF16),
                   jax.ShapeDtypeStruct((t, D_MODEL), BF16),
                   jax.ShapeDtypeStruct((D_MODEL, t), BF16)),
        grid=(t // tm, N_COLS),
        in_specs=[
            pl.BlockSpec((tm, D_MODEL), lambda m, n: (m, 0)),
            pl.BlockSpec((1, D_MODEL), lambda m, n: (0, 0)),
            pl.BlockSpec((D_MODEL, D_MODEL), lambda m, n: (0, n)),
            pl.BlockSpec((1, D_MODEL), lambda m, n: (0, 0)),
            pl.BlockSpec((1, D_MODEL), lambda m, n: (0, 0)),
        ],
        out_specs=(pl.BlockSpec((tm, D_MODEL), out_map),
                   pl.BlockSpec((tm, D_MODEL), lambda m, n: (m, 0)),
                   pl.BlockSpec((D_MODEL, tm), lambda m, n: (0, m))),
        scratch_shapes=[pltpu.VMEM((tm, D_MODEL), BF16), pltpu.VMEM((tm, D_MODEL), BF16)],
        compiler_params=pltpu.CompilerParams(
            dimension_semantics=("arbitrary", "arbitrary"), vmem_limit_bytes=VMEM_LIMIT),
        name="proj",
    )(x2d, norm_g, w_in, qg, kg)


def _attn_kernel(rb_ref, q_ref, k1_ref, k2_ref, vt_ref, dl_ref, lq1_ref, lk1_ref, lq2_ref, lk2_ref,
                 sg_ref, o_ref, s_scr, p_scr, q_scr, m_scr, c_scr, *, lam_init, nq, n_pairs):
    t = ATT_TILE
    step = pl.program_id(0)

    @pl.when(step == 0)
    def _():
        s_scr[...] = jnp.zeros(s_scr.shape, BF16)
        p_scr[...] = jnp.zeros(p_scr.shape, BF16)
        m_scr[...] = jnp.zeros(m_scr.shape, BF16)
        c_scr[...] = jnp.zeros(c_scr.shape, F32)

    pair_a = jnp.minimum(step, n_pairs - 1)
    pair_c = jnp.maximum(step - 2, 0)
    ha = lax.rem(pair_a // (nq // ATT_PAIR), N_HEADS)
    lam = (jnp.exp(jnp.sum(lq1_ref[...] * lk1_ref[...], axis=-1, keepdims=True))
           - jnp.exp(jnp.sum(lq2_ref[...] * lk2_ref[...], axis=-1, keepdims=True))
           + lam_init)

    def slabs(x, rows):
        return x.reshape(x.shape[0] // rows, rows, x.shape[1])

    def key_tile(i, r):
        kt = i + r
        kt = jnp.where(kt >= nq, kt - nq, kt)
        return pl.ds(pl.multiple_of(kt * t, t), t)

    def prepare(u):
        ia = lax.rem(pair_a * ATT_PAIR + u, nq)
        ic = lax.rem(pair_c * ATT_PAIR + u, nq)

        q = q_ref[u * t:(u + 1) * t, :]
        lane = lax.broadcasted_iota(jnp.int32, (t, LANES), 1)
        for side, bucket in enumerate((FAR_NEG_BUCKET, FAR_POS_BUCKET)):
            cv = jnp.full((t, LANES), rb_ref[bucket, ha] * LOG2E, F32)
            hi = cv.astype(BF16).astype(F32)
            for comp in range(2):
                base = HEAD_DIM if comp == 0 else 0
                keep = (lane < HEAD_DIM) if comp == 0 else (lane >= HEAD_DIM)
                aug = jnp.where(lane == base, hi, jnp.where(lane == base + 1, cv - hi, 0.0))
                q_scr[u, comp, side] = jnp.where(keep, q, aug.astype(BF16))

        mb1 = m_scr[u, 0]
        mb2 = m_scr[u, 1]
        c1 = c_scr[u, 0]
        rho = jnp.concatenate([c_scr[u, 1], c_scr[u, 1]], axis=0).astype(BF16)

        m0 = jnp.full((F32_ROWS, t), -jnp.inf, F32)
        l0 = jnp.zeros((BF16_ROWS, t), F32)
        return (ia, ic, mb1, mb2, c1, rho), (m0, m0, l0, l0, jnp.zeros((V_DIM, t), F32))

    def tile(u, r, consts, carry):
        ia, ic, mb1, mb2, _, rho = consts
        m1, m2, l1, l2, acc = carry
        a = slabs(p_scr[u, 0, r], BF16_ROWS) - slabs(p_scr[u, 1, r], BF16_ROWS) * rho
        acc = acc + _dot(vt_ref[:, key_tile(ic, r)], a.reshape(t, t))

        def probs(comp, mb):
            p = jnp.exp2((slabs(s_scr[u, comp, r], BF16_ROWS) - mb).reshape(t, t))
            p_scr[u, comp, r] = p
            parts = [p[i * BF16_ROWS:(i + 1) * BF16_ROWS, :] for i in range(t // BF16_ROWS)]
            while len(parts) > 1:
                parts = [parts[i] + parts[i + 1] for i in range(0, len(parts), 2)]
            return parts[0].astype(F32)

        l1 = l1 + probs(0, mb1)
        l2 = l2 + probs(1, mb2)
        wrapped = ia + r >= nq
        side = 1 if r == 0 else jnp.where(wrapped, 0, 1)
        keys = key_tile(ia, r)
        s1 = _dot_nt(k1_ref[keys, :], q_scr[u, 0, side])
        s2 = _dot_nt(k2_ref[keys, :], q_scr[u, 1, side])
        if r in (0, 1, nq - 1):
            offset = (r + 1) % nq - 1
            dl = dl_ref[0, jnp.where(wrapped == (r == nq - 1), offset + 1, 3)]
            s1 = s1 + dl
            s2 = s2 + dl
        s_scr[u, 0, r] = s1.astype(BF16)
        s_scr[u, 1, r] = s2.astype(BF16)
        m1 = jnp.maximum(m1, jnp.max(slabs(s1, F32_ROWS), axis=0))
        m2 = jnp.maximum(m2, jnp.max(slabs(s2, F32_ROWS), axis=0))
        return m1, m2, l1, l2, acc

    def finish(u, consts, carry):
        c1 = consts[4]
        m1, m2, l1, l2, acc = carry
        for comp, m in enumerate((m1, m2)):
            m_scr[u, comp] = jnp.broadcast_to(
                jnp.max(m, axis=0, keepdims=True), (BF16_ROWS, t)).astype(BF16)

        l1 = jnp.sum(l1, axis=0, keepdims=True)
        l2 = jnp.sum(l2, axis=0, keepdims=True)
        c_scr[u, 0] = jnp.broadcast_to(1.0 / l1, (F32_ROWS, t))
        c_scr[u, 1] = jnp.broadcast_to(lam * l1 / l2, (F32_ROWS, t))

        ot = (slabs(acc, F32_ROWS) * c1).reshape(V_DIM, t)
        ms = jnp.mean(ot * ot, axis=0, keepdims=True)
        ot = ot * lax.rsqrt(ms + EPS) * sg_ref[...] * (1.0 - lam_init)
        o_ref[u * t:(u + 1) * t, :] = ot.T.astype(BF16)

    blocks = [prepare(u) for u in range(ATT_PAIR)]
    for u, (consts, carry) in enumerate(blocks):
        for r in range(nq):
            carry = tile(u, r, consts, carry)
        finish(u, consts, carry)


def _attn(rel_bias, proj, k2a, vt, bias_tiles, lq1, lk1, lq2, lk2, subln_g, *, batch, seq, lam_init):
    t = ATT_TILE
    nq = seq // t
    assert nq >= 3 and nq % ATT_PAIR == 0
    ppq = nq // ATT_PAIR
    n_pairs = batch * N_HEADS * ppq

    def split(pair):
        bh = pair // ppq
        return bh // N_HEADS, lax.rem(bh, N_HEADS), lax.rem(pair, ppq)

    def a_pair(s):
        return split(jnp.minimum(s, n_pairs - 1))

    def c_pair(s):
        return split(jnp.maximum(s - 2, 0))

    def q_map(s):
        b, h, j = a_pair(s)
        return b * ppq + j, h

    def k1_map(s):
        b, h, _ = a_pair(s)
        return b, N_HEADS + h

    def k2_map(s):
        b, h, _ = a_pair(s)
        return b, h

    def v_map(s):
        b, h, _ = c_pair(s)
        return h, b

    def dl_map(s):
        return a_pair(s)[1], 0, 0, 0

    def o_map(s):
        b, h, j = c_pair(s)
        return b * ppq + j, h

    small = lambda shape: pl.BlockSpec(shape, lambda s: (0, 0))
    return pl.pallas_call(
        functools.partial(_attn_kernel, lam_init=lam_init, nq=nq, n_pairs=n_pairs),
        out_shape=jax.ShapeDtypeStruct((batch * seq, N_HEADS * V_DIM), BF16),
        grid=(n_pairs + 2,),
        in_specs=[
            pl.BlockSpec(memory_space=pltpu.SMEM),
            pl.BlockSpec((ATT_PAIR * t, LANES), q_map),
            pl.BlockSpec((seq, LANES), k1_map),
            pl.BlockSpec((seq, LANES), k2_map),
            pl.BlockSpec((V_DIM, seq), v_map),
            pl.BlockSpec((1, 4, t, t), dl_map),
            small((1, HEAD_DIM)), small((1, HEAD_DIM)), small((1, HEAD_DIM)), small((1, HEAD_DIM)),
            small((V_DIM, 1)),
        ],
        out_specs=pl.BlockSpec((ATT_PAIR * t, V_DIM), o_map),
        scratch_shapes=[
            pltpu.VMEM((ATT_PAIR, 2, nq, t, t), BF16),
            pltpu.VMEM((ATT_PAIR, 2, nq, t, t), BF16),
            pltpu.VMEM((ATT_PAIR, 2, 2, t, LANES), BF16),
            pltpu.VMEM((ATT_PAIR, 2, BF16_ROWS, t), BF16),
            pltpu.VMEM((ATT_PAIR, 2, F32_ROWS, t), F32),
        ],
        compiler_params=pltpu.CompilerParams(
            dimension_semantics=("arbitrary",), vmem_limit_bytes=VMEM_LIMIT),
        name="attn",
    )(rel_bias, proj, proj, k2a, vt, bias_tiles, lq1, lk1, lq2, lk2, subln_g)


def _post_kernel(o_ref, sga_ref, pc_ref, bg_ref, sma_ref, smb_ref, pcp_ref, pcn_ref, x_ref, cw_ref,
                 wa_ref, wc_ref, wo_ref, out_ref, *, tiles_per_seq):
    tm = out_ref.shape[0]
    pos = lax.rem(pl.program_id(0), tiles_per_seq)

    pc = pc_ref[...].astype(F32)
    prev_row = pcp_ref[...].astype(F32)[BF16_ROWS - 1:BF16_ROWS, :]
    next_row = pcn_ref[...].astype(F32)[0:1, :]
    prev_row = jnp.where(pos == 0, 0.0, prev_row)
    next_row = jnp.where(pos == tiles_per_seq - 1, 0.0, next_row)
    row = lax.broadcasted_iota(jnp.int32, pc.shape, 0)
    up = jnp.where(row == 0, prev_row, pltpu.roll(pc, 1, axis=0))
    dn = jnp.where(row == tm - 1, next_row, pltpu.roll(pc, tm - 1, axis=0))
    cw = cw_ref[...]
    cv = up * cw[0:1, :] + pc * cw[1:2, :] + dn * cw[2:3, :]
    ub = (bg_ref[...].astype(F32) * cv).astype(BF16)

    for r0 in range(0, tm, POST_CHUNK_ROWS):
        rs = slice(r0, r0 + POST_CHUNK_ROWS)
        ua = (o_ref[rs, :].astype(F32) * sga_ref[rs, :].astype(F32)).astype(BF16)
        ya = _dot(ua, wa_ref[...])
        yb = _dot(ub[rs, :], wc_ref[...])
        merged = (sma_ref[rs, :].astype(F32) * ya + smb_ref[rs, :].astype(F32) * yb).astype(BF16)
        out_ref[rs, :] = x_ref[rs, :] + _dot(merged, wo_ref[...])


def _post(o, proj, x2d, conv_w, wa, wc, wo, *, seq):
    t = x2d.shape[0]
    tm = POST_ROWS
    hb = tm // BF16_ROWS
    n_hb = t // BF16_ROWS
    grp = OUT_GROUPS.index
    col = lambda c: pl.BlockSpec((tm, D_MODEL), lambda m: (m, c))
    prev = lambda c: pl.BlockSpec((BF16_ROWS, D_MODEL), lambda m: (jnp.maximum(m * hb - 1, 0), c))
    nxt = lambda c: pl.BlockSpec((BF16_ROWS, D_MODEL),
                                 lambda m: (jnp.minimum((m + 1) * hb, n_hb - 1), c))
    whole = lambda r: pl.BlockSpec((r, D_MODEL), lambda m: (0, 0))
    return pl.pallas_call(
        functools.partial(_post_kernel, tiles_per_seq=seq // tm),
        out_shape=jax.ShapeDtypeStruct((t, D_MODEL), F32),
        grid=(t // tm,),
        in_specs=[
            col(0),
            col(grp("silu_ga")), col(grp("c_xc")), col(grp("b_silu_gb")),
            col(grp("sig_ma")), col(grp("sig_mb")),
            prev(grp("c_xc")), nxt(grp("c_xc")),
            col(0),
            whole(3), whole(D_MODEL), whole(D_MODEL), whole(D_MODEL),
        ],
        out_specs=pl.BlockSpec((tm, D_MODEL), lambda m: (m, 0)),
        compiler_params=pltpu.CompilerParams(
            dimension_semantics=("arbitrary",), vmem_limit_bytes=VMEM_LIMIT),
        name="post",
    )(o, proj, proj, proj, proj, proj, proj, proj, x2d, conv_w, wa, wc, wo)


def _lambda_init(layer_idx):
    return 0.8 - 0.6 * math.exp(-0.3 * layer_idx)


def _trunk(x, bias_tiles, p):
    batch, seq, _ = x.shape
    x2d = x.reshape(batch * seq, D_MODEL)
    for l in range(DEPTH):
        proj, k2a, vt = _proj(x2d, p["norm_g"][l], p["w_in"][l], p["qg"][l], p["kg"][l])
        o = _attn(p["rel_bias"], proj, k2a, vt, bias_tiles, p["lq1"][l], p["lk1"][l], p["lq2"][l],
                  p["lk2"][l], p["subln_g"][l], batch=batch, seq=seq, lam_init=_lambda_init(l))
        x2d = _post(o, proj, x2d, p["conv_w"][l], p["wa"][l], p["wc"][l], p["wo"][l], seq=seq)
    return x2d.reshape(batch, seq, D_MODEL)


def kernel(x_prompt, x_sample, rel_bias, norm_g, w_in, q_norm_g, k_norm_g, lambda_q1, lambda_k1,
           lambda_q2, lambda_k2, subln_g, w_attn_out, conv_w, w_conv_out, w_o):
    reps = D_MODEL // HEAD_DIM
    p = {
        "rel_bias": rel_bias,
        "norm_g": norm_g.reshape(DEPTH, 1, D_MODEL),
        "w_in": w_in.astype(BF16),
        "qg": jnp.tile(q_norm_g, (1, reps)).reshape(DEPTH, 1, D_MODEL),
        "kg": jnp.tile(k_norm_g, (1, reps)).reshape(DEPTH, 1, D_MODEL),
        "lq1": lambda_q1.reshape(DEPTH, 1, HEAD_DIM),
        "lk1": lambda_k1.reshape(DEPTH, 1, HEAD_DIM),
        "lq2": lambda_q2.reshape(DEPTH, 1, HEAD_DIM),
        "lk2": lambda_k2.reshape(DEPTH, 1, HEAD_DIM),
        "subln_g": subln_g.reshape(DEPTH, V_DIM, 1),
        "conv_w": conv_w,
        "wa": w_attn_out.astype(BF16),
        "wc": w_conv_out.astype(BF16),
        "wo": w_o.astype(BF16),
    }
    bias_tiles = _bias_tiles(rel_bias)
    return (_trunk(x_prompt, bias_tiles, p), _trunk(x_sample, bias_tiles, p))
```
